```python
import math
import jax, jax.numpy as jnp
from jax import lax
import numpy as np

D_MODEL = 2048
BATCH = 4
SEQ = 4096
DEPTH = 1

N_HEADS = 8
HEAD_DIM = 64
ATTN_W = N_HEADS * 2 * HEAD_DIM
Q_BLOCK = 128
CONV_CH = 1024
CONV_WIDTH = 31
NUM_BUCKETS = 32
MAX_DISTANCE = 128
D_FF = 5632
N_IN = 3 * ATTN_W + 2 * CONV_CH + 2 * D_MODEL
NEG_INF = -1e30

kernel_name = "hybrid_diffattn_conformer_conv_gated"


def rms_norm(x, g, eps=1e-6):
    xf = x.astype(jnp.float32)
    y = xf * lax.rsqrt(jnp.mean(xf * xf, axis=-1, keepdims=True) + eps)
    return (y * g.astype(jnp.float32)).astype(x.dtype)


def layer_norm(x, g, b, eps=1e-5):
    xf = x.astype(jnp.float32)
    mu = jnp.mean(xf, axis=-1, keepdims=True)
    var = jnp.mean(jnp.square(xf - mu), axis=-1, keepdims=True)
    y = (xf - mu) * lax.rsqrt(var + eps)
    return (y * g.astype(jnp.float32) + b.astype(jnp.float32)).astype(x.dtype)


def swiglu(h, w_gate, w_up, w_down):
    return (jax.nn.silu(h @ w_gate) * (h @ w_up)) @ w_down


def t5_bucket(n):
    max_exact = NUM_BUCKETS // 2
    nf = jnp.maximum(n, 1).astype(jnp.float32)
    large = max_exact + (jnp.log(nf / max_exact) / math.log(MAX_DISTANCE / max_exact)
                         * (NUM_BUCKETS - max_exact)).astype(jnp.int32)
    large = jnp.minimum(large, NUM_BUCKETS - 1)
    return jnp.where(n < max_exact, n, large)


def diff_attention(q1, q2, k1, k2, v, lam, rel_bias_table):
    B, H, S, dh = q1.shape
    dv = v.shape[-1]
    nb = S // Q_BLOCK
    scale = dh ** -0.5
    kpos = jnp.arange(S, dtype=jnp.int32)

    def to_blocks(t):
        return t.reshape(B, H, nb, Q_BLOCK, dh).transpose(2, 0, 1, 3, 4)

    def one_block(args):
        qb1, qb2, start = args
        qpos = start + jnp.arange(Q_BLOCK, dtype=jnp.int32)
        rel = qpos[:, None] - kpos[None, :]
        causal = rel >= 0
        bias = jnp.transpose(rel_bias_table[t5_bucket(jnp.maximum(rel, 0))],
                             (2, 0, 1)).astype(jnp.float32)

        def probs(qb, k):
            s = jnp.einsum('bhqd,bhkd->bhqk', qb, k).astype(jnp.float32) * scale + bias
            s = jnp.where(causal, s, NEG_INF)
            return jax.nn.softmax(s, axis=-1)

        a = probs(qb1, k1) - lam * probs(qb2, k2)
        return jnp.einsum('bhqk,bhkd->bhqd', a.astype(v.dtype), v)

    starts = jnp.arange(nb, dtype=jnp.int32) * Q_BLOCK
    out = lax.map(one_block, (to_blocks(q1), to_blocks(q2), starts))
    return out.transpose(1, 0, 3, 2, 4).reshape(B, S, H, dv)


def causal_depthwise_conv(u, w, b):
    C = u.shape[-1]
    y = lax.conv_general_dilated(
        u, w[:, None, :].astype(u.dtype), window_strides=(1,),
        padding=[(CONV_WIDTH - 1, 0)],
        dimension_numbers=('NWC', 'WIO', 'NWC'),
        feature_group_count=C)
    return y + b


def setup_inputs(seed: int = 0) -> dict:
    key = jax.random.key(seed)
    ks = jax.random.split(key, 32)
    f32 = jnp.float32

    def nrm(k, shape, scale):
        return jax.random.normal(k, shape, f32) * scale

    def gain(k, shape):
        return 1.0 + 0.02 * jax.random.normal(k, shape, f32)

    L, D = DEPTH, D_MODEL
    return {
        "x": nrm(ks[0], (BATCH, SEQ, D), 1.0),
        "ffn1_norm_g": gain(ks[1], (L, D)),
        "ffn1_w_gate": nrm(ks[2], (L, D, D_FF), D ** -0.5),
        "ffn1_w_up": nrm(ks[3], (L, D, D_FF), D ** -0.5),
        "ffn1_w_down": nrm(ks[4], (L, D_FF, D), D_FF ** -0.5),
        "mix_norm_g": gain(ks[5], (L, D)),
        "w_in": nrm(ks[6], (L, D, N_IN), D ** -0.5),
        "rel_bias_table": nrm(ks[7], (NUM_BUCKETS, N_HEADS), 0.5),
        "lambda_q1": nrm(ks[8], (L, HEAD_DIM), 0.1),
        "lambda_k1": nrm(ks[9], (L, HEAD_DIM), 0.1),
        "lambda_q2": nrm(ks[10], (L, HEAD_DIM), 0.1),
        "lambda_k2": nrm(ks[11], (L, HEAD_DIM), 0.1),
        "attn_head_norm_g": gain(ks[12], (L, 2 * HEAD_DIM)),
        "w_attn_branch": nrm(ks[13], (L, ATTN_W, D), ATTN_W ** -0.5),
        "conv_dw_w": nrm(ks[14], (L, CONV_WIDTH, CONV_CH), CONV_WIDTH ** -0.5),
        "conv_dw_b": nrm(ks[15], (L, CONV_CH), 0.02),
        "conv_ln_g": gain(ks[16], (L, CONV_CH)),
        "conv_ln_b": nrm(ks[17], (L, CONV_CH), 0.02),
        "w_conv_branch": nrm(ks[18], (L, CONV_CH, D), CONV_CH ** -0.5),
        "w_out": nrm(ks[19], (L, D, D), D ** -0.5),
        "ffn2_norm_g": gain(ks[20], (L, D)),
        "ffn2_w_gate": nrm(ks[21], (L, D, D_FF), D ** -0.5),
        "ffn2_w_up": nrm(ks[22], (L, D, D_FF), D ** -0.5),
        "ffn2_w_down": nrm(ks[23], (L, D_FF, D), D_FF ** -0.5),
        "final_norm_g": gain(ks[24], (D,)),
    }


def reference(x, ffn1_norm_g, ffn1_w_gate, ffn1_w_up, ffn1_w_down, mix_norm_g, w_in,
              rel_bias_table, lambda_q1, lambda_k1, lambda_q2, lambda_k2, attn_head_norm_g,
              w_attn_branch, conv_dw_w, conv_dw_b, conv_ln_g, conv_ln_b, w_conv_branch,
              w_out, ffn2_norm_g, ffn2_w_gate, ffn2_w_up, ffn2_w_down, final_norm_g):
    B, S, D = x.shape
    splits = np.cumsum([ATTN_W, ATTN_W, ATTN_W, CONV_CH, CONV_CH, D_MODEL]).tolist()
    for l in range(DEPTH):
        lam_init = 0.8 - 0.6 * math.exp(-0.3 * l)

        x = x + 0.5 * swiglu(rms_norm(x, ffn1_norm_g[l]), ffn1_w_gate[l], ffn1_w_up[l], ffn1_w_down[l])

        h = rms_norm(x, mix_norm_g[l])
        z = h @ w_in[l]
        q, k, v, glu_a, glu_b, gate_a, gate_c = jnp.split(z, splits, axis=-1)

        q = q.reshape(B, S, N_HEADS, 2, HEAD_DIM).transpose(3, 0, 2, 1, 4)
        k = k.reshape(B, S, N_HEADS, 2, HEAD_DIM).transpose(3, 0, 2, 1, 4)
        v = v.reshape(B, S, N_HEADS, 2 * HEAD_DIM).transpose(0, 2, 1, 3)
        lq1 = lambda_q1[l].astype(jnp.float32)
        lk1 = lambda_k1[l].astype(jnp.float32)
        lq2 = lambda_q2[l].astype(jnp.float32)
        lk2 = lambda_k2[l].astype(jnp.float32)
        lam = jnp.exp(jnp.sum(lq1 * lk1)) - jnp.exp(jnp.sum(lq2 * lk2)) + lam_init
        o = diff_attention(q[0], q[1], k[0], k[1], v, lam, rel_bias_table)
        o = rms_norm(o, attn_head_norm_g[l], eps=1e-5) * (1.0 - lam_init)
        attn_branch = o.reshape(B, S, ATTN_W) @ w_attn_branch[l]

        u = glu_a * jax.nn.sigmoid(glu_b)
        u = causal_depthwise_conv(u, conv_dw_w[l], conv_dw_b[l])
        u = jax.nn.silu(layer_norm(u, conv_ln_g[l], conv_ln_b[l]))
        conv_branch = u @ w_conv_branch[l]

        merged = jax.nn.sigmoid(gate_a) * attn_branch + jax.nn.sigmoid(gate_c) * conv_branch
        x = x + merged @ w_out[l]

        x = x + 0.5 * swiglu(rms_norm(x, ffn2_norm_g[l]), ffn2_w_gate[l], ffn2_w_up[l], ffn2_w_down[l])

    return rms_norm(x, final_norm_g)
```

```python
import functools
import math

import jax
import jax.numpy as jnp
from jax import lax
from jax.experimental import pallas as pl
from jax.experimental.pallas import tpu as pltpu

F32 = jnp.float32
BF16 = jnp.bfloat16

N_HEADS = 8
HEAD_DIM = 64
HEAD_W = 2 * HEAD_DIM
CONV_WIDTH = 31
NUM_BUCKETS = 32
MAX_EXACT = NUM_BUCKETS // 2
MAX_DISTANCE = 128
NEG_INF = -1e30
LANES = 128
MIB = 1024 * 1024
VMEM_LIMIT = 56 * MIB

ATT_BLK = 128
ATT_FAR = 512
CONV_HALO = 32
CONV_ROWS = 16


def _rms(x, g, eps):
    return x * lax.rsqrt(jnp.mean(x * x, axis=-1, keepdims=True) + eps) * g


def _sigmoid(x):
    return 1.0 / (1.0 + jnp.exp(-x))


def _ffn_body(x_ref, g_ref, wg_ref, wu_ref, wd_ref, ng_ref, *rest, final):
    if final:
        o_ref, h_ref = rest
    else:
        o_ref, hn_ref, h_ref = rest
    f = pl.program_id(1)

    @pl.when(f == 0)
    def _():
        x = x_ref[...]
        h_ref[...] = _rms(x, g_ref[...], 1e-6).astype(BF16)
        o_ref[...] = x

    h = h_ref[...]
    gate = jnp.dot(h, wg_ref[...], preferred_element_type=F32)
    up = jnp.dot(h, wu_ref[...], preferred_element_type=F32)
    act = (gate * _sigmoid(gate)) * up * 0.5
    o_ref[...] += jnp.dot(act.astype(BF16), wd_ref[...], preferred_element_type=F32)

    @pl.when(f == pl.num_programs(1) - 1)
    def _():
        y = _rms(o_ref[...], ng_ref[...], 1e-6)
        if final:
            o_ref[...] = y
        else:
            hn_ref[...] = y.astype(BF16)


def _ffn(x, g, wg, wu, wd, ng, *, final, bm=512, tf=512):
    t, d = x.shape
    dff = wg.shape[1]
    grid = (t // bm, dff // tf)
    row = lambda i, f: (i, 0)
    out_shape = [jax.ShapeDtypeStruct((t, d), F32)]
    out_specs = [pl.BlockSpec((bm, d), row)]
    if not final:
        out_shape.append(jax.ShapeDtypeStruct((t, d), BF16))
        out_specs.append(pl.BlockSpec((bm, d), row))
    res = pl.pallas_call(
        functools.partial(_ffn_body, final=final),
        grid=grid,
        in_specs=[
            pl.BlockSpec((bm, d), row),
            pl.BlockSpec((1, d), lambda i, f: (0, 0)),
            pl.BlockSpec((d, tf), lambda i, f: (0, f)),
            pl.BlockSpec((d, tf), lambda i, f: (0, f)),
            pl.BlockSpec((tf, d), lambda i, f: (f, 0)),
            pl.BlockSpec((1, d), lambda i, f: (0, 0)),
        ],
        out_specs=out_specs,
        out_shape=out_shape,
        scratch_shapes=[pltpu.VMEM((bm, d), BF16)],
        compiler_params=pltpu.CompilerParams(
            dimension_semantics=("parallel", "arbitrary"), vmem_limit_bytes=VMEM_LIMIT),
        name="ffn_final" if final else "ffn",
    )(x, g, wg, wu, wd, ng)
    return res[0] if final else res


def _mm_body(a_ref, b_ref, o_ref):
    o_ref[...] = jnp.dot(a_ref[...], b_ref[...], preferred_element_type=F32).astype(o_ref.dtype)


def _mm(a, b, out_dtype, *, bm=1024, tn=1024, name):
    m, k = a.shape
    n = b.shape[1]
    return pl.pallas_call(
        _mm_body,
        grid=(m // bm, n // tn),
        in_specs=[pl.BlockSpec((bm, k), lambda i, j: (i, 0)),
                  pl.BlockSpec((k, tn), lambda i, j: (0, j))],
        out_specs=pl.BlockSpec((bm, tn), lambda i, j: (i, j)),
        out_shape=jax.ShapeDtypeStruct((m, n), out_dtype),
        compiler_params=pltpu.CompilerParams(
            dimension_semantics=("parallel", "arbitrary"), vmem_limit_bytes=VMEM_LIMIT),
        name=name,
    )(a, b)


def _mm_nt_body(wt_ref, a_ref, o_ref):
    o_ref[...] = lax.dot_general(wt_ref[...], a_ref[...], (((1,), (1,)), ((), ())),
                                 preferred_element_type=F32).astype(o_ref.dtype)


def _mm_nt(wt, a, out_dtype, *, bm=1024, name):
    n, k = wt.shape
    m = a.shape[0]
    return pl.pallas_call(
        _mm_nt_body,
        grid=(m // bm,),
        in_specs=[pl.BlockSpec((n, k), lambda i: (0, 0)),
                  pl.BlockSpec((bm, k), lambda i: (i, 0))],
        out_specs=pl.BlockSpec((n, bm), lambda i: (0, i)),
        out_shape=jax.ShapeDtypeStruct((n, m), out_dtype),
        compiler_params=pltpu.CompilerParams(
            dimension_semantics=("parallel",), vmem_limit_bytes=VMEM_LIMIT),
        name=name,
    )(wt, a)


def _t5_bias(d, tbl_ref, h):
    n = jnp.maximum(d, 0)
    nf = jnp.maximum(n, 1).astype(F32)
    large = MAX_EXACT + (jnp.log(nf / MAX_EXACT) / math.log(MAX_DISTANCE / MAX_EXACT)
                         * (NUM_BUCKETS - MAX_EXACT)).astype(jnp.int32)
    large = jnp.minimum(large, NUM_BUCKETS - 1)
    bucket = jnp.where(n < MAX_EXACT, n, large)
    b = jnp.zeros(d.shape, F32)
    for j in range(NUM_BUCKETS):
        b = jnp.where(bucket == j, tbl_ref[j, h], b)
    b = b - tbl_ref[NUM_BUCKETS - 1, h]
    return jnp.where(d >= 0, b, NEG_INF)


def _attn_body(tbl_ref, lam_ref, g_ref, q_ref, k_ref, vt_ref, o_ref, bias_ref, *, lam_init, seq):
    h = pl.program_id(1)
    blk = ATT_BLK
    kk = lax.broadcasted_iota(jnp.int32, (blk, blk), 0)
    jj = lax.broadcasted_iota(jnp.int32, (blk, blk), 1)
    diag = _t5_bias(jj - kk, tbl_ref, h)
    sub = _t5_bias(blk + jj - kk, tbl_ref, h)
    bias_ref[0] = jnp.concatenate([diag, diag], axis=1)
    bias_ref[1] = jnp.concatenate([sub, sub], axis=1)

    lv = lam_ref[...]
    lam = (jnp.exp(jnp.sum(lv[0:1] * lv[1:2], axis=-1, keepdims=True))
           - jnp.exp(jnp.sum(lv[2:3] * lv[3:4], axis=-1, keepdims=True)) + lam_init)
    g = g_ref[...]
    lane = lax.broadcasted_iota(jnp.int32, (blk, HEAD_W), 1)
    scale = HEAD_DIM ** -0.5

    def q_block(qi, carry):
        q0 = pl.multiple_of(qi * blk, blk)
        q = q_ref[pl.ds(q0, blk), :].astype(F32) * scale
        qs = jnp.concatenate([jnp.where(lane < HEAD_DIM, q, 0.0),
                              jnp.where(lane >= HEAD_DIM, q, 0.0)], axis=0).astype(BF16)

        def step(k0, bk, bias, state):
            m, l, acc = state
            kb = k_ref[pl.ds(k0, bk), :]
            st = lax.dot_general(kb, qs, (((1,), (1,)), ((), ())), preferred_element_type=F32)
            if bias is not None:
                st = st + bias
            m_new = jnp.maximum(m, jnp.max(st, axis=0, keepdims=True))
            alpha = jnp.exp(m - m_new)
            p = jnp.exp(st - m_new)
            l = alpha * l + jnp.sum(p, axis=0, keepdims=True)
            vt = vt_ref[:, pl.ds(k0, bk)]
            acc = alpha * acc + jnp.dot(vt, p.astype(BF16), preferred_element_type=F32)
            return m_new, l, acc

        state = (jnp.full((1, 2 * blk), NEG_INF, F32), jnp.zeros((1, 2 * blk), F32),
                 jnp.zeros((HEAD_W, 2 * blk), F32))
        n_far = jnp.maximum(qi - 1, 0)
        per = ATT_FAR // blk
        n_big = n_far // per
        state = lax.fori_loop(
            0, n_big,
            lambda i, s: step(pl.multiple_of(i * ATT_FAR, ATT_FAR), ATT_FAR, None, s), state)
        state = lax.fori_loop(
            n_big * per, n_far,
            lambda i, s: step(pl.multiple_of(i * blk, blk), blk, None, s), state)
        state = lax.fori_loop(
            0, jnp.minimum(qi, 1),
            lambda i, s: step(pl.multiple_of(q0 - blk, blk), blk, bias_ref[1], s), state)
        m, l, acc = step(q0, blk, bias_ref[0], state)

        o = acc * (1.0 / l)
        ot = o[:, :blk] - lam * o[:, blk:]
        y = ot * lax.rsqrt(jnp.mean(ot * ot, axis=0, keepdims=True) + 1e-5) * g
        y = y * (1.0 - lam_init)
        o_ref[pl.ds(q0, blk), :] = y.T.astype(o_ref.dtype)
        return carry

    lax.fori_loop(0, seq // blk, q_block, 0)


def _attention(qk, vt, tbl, lam_vecs, g_col, *, batch, seq, lam_init):
    t = qk.shape[0]
    return pl.pallas_call(
        functools.partial(_attn_body, lam_init=lam_init, seq=seq),
        grid=(batch, N_HEADS),
        in_specs=[
            pl.BlockSpec(memory_space=pltpu.SMEM),
            pl.BlockSpec((4, HEAD_DIM), lambda b, h: (0, 0)),
            pl.BlockSpec((HEAD_W, 1), lambda b, h: (0, 0)),
            pl.BlockSpec((seq, HEAD_W), lambda b, h: (b, h)),
            pl.BlockSpec((seq, HEAD_W), lambda b, h: (b, N_HEADS + h)),
            pl.BlockSpec((HEAD_W, seq), lambda b, h: (h, b)),
        ],
        out_specs=pl.BlockSpec((seq, HEAD_W), lambda b, h: (b, h)),
        out_shape=jax.ShapeDtypeStruct((t, N_HEADS * HEAD_W), BF16),
        scratch_shapes=[pltpu.VMEM((2, ATT_BLK, 2 * ATT_BLK), F32)],
        compiler_params=pltpu.CompilerParams(
            dimension_semantics=("parallel", "parallel"), vmem_limit_bytes=VMEM_LIMIT),
        name="diff_attn",
    )(tbl, lam_vecs, g_col, qk, qk, vt)


def _conv_body(cur_ref, halo_ref, w_ref, b_ref, g_ref, beta_ref, o_ref, ext_ref, *, blocks_per_seq):
    i = pl.program_id(0)
    c = o_ref.shape[1]
    bs = o_ref.shape[0]
    cur = cur_ref[...]
    ext_ref[CONV_HALO:, :] = cur[:, :c] * _sigmoid(cur[:, c:])
    hal = halo_ref[...]
    uh = hal[:, :c] * _sigmoid(hal[:, c:])
    ext_ref[:CONV_HALO, :] = jnp.where(i % blocks_per_seq == 0, 0.0, uh)

    def rows(r, carry):
        r0 = pl.multiple_of(r * CONV_ROWS, CONV_ROWS)
        win = ext_ref[pl.ds(r0, CONV_ROWS + CONV_HALO), :]
        acc = jnp.zeros((CONV_ROWS, c), F32)
        for j in range(CONV_WIDTH):
            off = CONV_HALO - (CONV_WIDTH - 1) + j
            acc = acc + w_ref[j:j + 1, :] * win[off:off + CONV_ROWS, :]
        y = acc + b_ref[...]
        mu = jnp.mean(y, axis=-1, keepdims=True)
        var = jnp.mean(jnp.square(y - mu), axis=-1, keepdims=True)
        yn = (y - mu) * lax.rsqrt(var + 1e-5) * g_ref[...] + beta_ref[...]
        o_ref[pl.ds(r0, CONV_ROWS), :] = (yn * _sigmoid(yn)).astype(o_ref.dtype)
        return carry

    lax.fori_loop(0, bs // CONV_ROWS, rows, 0)


def _conv_branch(glu, w, b, g, beta, *, seq, bs=512):
    t, c2 = glu.shape
    c = c2 // 2
    per = bs // CONV_HALO
    vec = lambda i: (0, 0)
    return pl.pallas_call(
        functools.partial(_conv_body, blocks_per_seq=seq // bs),
        grid=(t // bs,),
        in_specs=[
            pl.BlockSpec((bs, c2), lambda i: (i, 0)),
            pl.BlockSpec((CONV_HALO, c2), lambda i: (jnp.maximum(i * per - 1, 0), 0)),
            pl.BlockSpec((CONV_WIDTH, c), vec),
            pl.BlockSpec((1, c), vec),
            pl.BlockSpec((1, c), vec),
            pl.BlockSpec((1, c), vec),
        ],
        out_specs=pl.BlockSpec((bs, c), lambda i: (i, 0)),
        out_shape=jax.ShapeDtypeStruct((t, c), BF16),
        scratch_shapes=[pltpu.VMEM((CONV_HALO + bs, c), F32)],
        compiler_params=pltpu.CompilerParams(
            dimension_semantics=("parallel",), vmem_limit_bytes=VMEM_LIMIT),
        name="conv_branch",
    )(glu, glu, w, b, g, beta)


def _merge_body(x_ref, a_ref, u_ref, gate_ref, wa_ref, wc_ref, wo_ref, o_ref):
    d = o_ref.shape[1]
    ab = jnp.dot(a_ref[...], wa_ref[...], preferred_element_type=F32)
    cb = jnp.dot(u_ref[...], wc_ref[...], preferred_element_type=F32)
    gates = gate_ref[...]
    merged = _sigmoid(gates[:, :d]) * ab + _sigmoid(gates[:, d:]) * cb
    o_ref[...] = x_ref[...] + jnp.dot(merged.astype(BF16), wo_ref[...],
                                      preferred_element_type=F32)


def _merge(x1, attn_o, conv_u, gates, wa, wc, wo, *, bm=256):
    t, d = x1.shape
    row = lambda i: (i, 0)
    const = lambda i: (0, 0)
    resident = dict(pipeline_mode=pl.Buffered(1))
    return pl.pallas_call(
        _merge_body,
        grid=(t // bm,),
        in_specs=[
            pl.BlockSpec((bm, d), row),
            pl.BlockSpec((bm, attn_o.shape[1]), row),
            pl.BlockSpec((bm, conv_u.shape[1]), row),
            pl.BlockSpec((bm, 2 * d), row),
            pl.BlockSpec(wa.shape, const, **resident),
            pl.BlockSpec(wc.shape, const, **resident),
            pl.BlockSpec(wo.shape, const, **resident),
        ],
        out_specs=pl.BlockSpec((bm, d), row),
        out_shape=jax.ShapeDtypeStruct((t, d), F32),
        compiler_params=pltpu.CompilerParams(
            dimension_semantics=("parallel",), vmem_limit_bytes=VMEM_LIMIT),
        name="merge",
    )(x1, attn_o, conv_u, gates, wa, wc, wo)


def kernel(x, ffn1_norm_g, ffn1_w_gate, ffn1_w_up, ffn1_w_down, mix_norm_g, w_in, rel_bias_table, lambda_q1, lambda_k1, lambda_q2, lambda_k2, attn_head_norm_g, w_attn_branch, conv_dw_w, conv_dw_b, conv_ln_g, conv_ln_b, w_conv_branch, w_out, ffn2_norm_g, ffn2_w_gate, ffn2_w_up, ffn2_w_down, final_norm_g):
    batch, seq, d = x.shape
    depth = ffn1_w_gate.shape[0]
    attn_w = N_HEADS * HEAD_W
    conv_ch = conv_dw_w.shape[2]
    xt = x.reshape(batch * seq, d)
    row = lambda v: v.reshape(1, -1).astype(F32)

    for l in range(depth):
        lam_init = 0.8 - 0.6 * math.exp(-0.3 * l)
        last = l == depth - 1
        w_in_l = w_in[l]
        w_qk = w_in_l[:, :2 * attn_w].astype(BF16)
        w_vt = w_in_l[:, 2 * attn_w:3 * attn_w].T.astype(BF16)
        w_glu = w_in_l[:, 3 * attn_w:3 * attn_w + 2 * conv_ch].astype(BF16)
        w_gates = w_in_l[:, 3 * attn_w + 2 * conv_ch:].astype(BF16)

        x1, h2 = _ffn(xt, row(ffn1_norm_g[l]), ffn1_w_gate[l].astype(BF16),
                      ffn1_w_up[l].astype(BF16), ffn1_w_down[l].astype(BF16),
                      row(mix_norm_g[l]), final=False)

        qk = _mm(h2, w_qk, BF16, name="proj_qk")
        vt = _mm_nt(w_vt, h2, BF16, name="proj_vt")
        glu = _mm(h2, w_glu, F32, name="proj_glu")
        gates = _mm(h2, w_gates, F32, name="proj_gates")

        lam_vecs = jnp.stack([lambda_q1[l], lambda_k1[l], lambda_q2[l], lambda_k2[l]]).astype(F32)
        attn_o = _attention(qk, vt, rel_bias_table.astype(F32), lam_vecs,
                            attn_head_norm_g[l].reshape(HEAD_W, 1).astype(F32),
                            batch=batch, seq=seq, lam_init=lam_init)
        conv_u = _conv_branch(glu, conv_dw_w[l].astype(F32), row(conv_dw_b[l]),
                              row(conv_ln_g[l]), row(conv_ln_b[l]), seq=seq)
        x2 = _merge(x1, attn_o, conv_u, gates, w_attn_branch[l].astype(BF16),
                    w_conv_branch[l].astype(BF16), w_out[l].astype(BF16))

        res = _ffn(x2, row(ffn2_norm_g[l]), ffn2_w_gate[l].astype(BF16),
                   ffn2_w_up[l].astype(BF16), ffn2_w_down[l].astype(BF16),
                   row(final_norm_g), final=last)
        xt = res if last else res[0]

    return xt.reshape(batch, seq, d)
```

```python
import functools
import math

import jax
import jax.numpy as jnp
from jax import lax
from jax.experimental import pallas as pl
from jax.experimental.pallas import tpu as pltpu

F32 = jnp.float32
BF16 = jnp.bfloat16

N_HEADS = 8
HEAD_DIM = 64
HEAD_W = 2 * HEAD_DIM
CONV_WIDTH = 31
NUM_BUCKETS = 32
MAX_EXACT = NUM_BUCKETS // 2
MAX_DISTANCE = 128
NEG_INF = -1e30
LANES = 128
MIB = 1024 * 1024
VMEM_LIMIT = 56 * MIB

ATT_BLK = 128
ATT_GROUP = 4
CONV_HALO = 32
CONV_ROWS = 16


def _rms(x, g, eps):
    return x * lax.rsqrt(jnp.mean(x * x, axis=-1, keepdims=True) + eps) * g


def _sigmoid(x):
    return 1.0 / (1.0 + jnp.exp(-x))


def _ffn_body(x_ref, g_ref, wg_ref, wu_ref, wd_ref, ng_ref, *rest, final):
    if final:
        o_ref, h_ref = rest
    else:
        o_ref, hn_ref, h_ref = rest
    f = pl.program_id(1)

    @pl.when(f == 0)
    def _():
        x = x_ref[...]
        h_ref[...] = _rms(x, g_ref[...], 1e-6).astype(BF16)
        o_ref[...] = x

    h = h_ref[...]
    gate = jnp.dot(h, wg_ref[...], preferred_element_type=F32)
    up = jnp.dot(h, wu_ref[...], preferred_element_type=F32)
    act = (gate * _sigmoid(gate)) * up * 0.5
    o_ref[...] += jnp.dot(act.astype(BF16), wd_ref[...], preferred_element_type=F32)

    @pl.when(f == pl.num_programs(1) - 1)
    def _():
        y = _rms(o_ref[...], ng_ref[...], 1e-6)
        if final:
            o_ref[...] = y
        else:
            hn_ref[...] = y.astype(BF16)


def _ffn(x, g, wg, wu, wd, ng, *, final, bm=512, tf=512):
    t, d = x.shape
    dff = wg.shape[1]
    grid = (t // bm, dff // tf)
    row = lambda i, f: (i, 0)
    out_shape = [jax.ShapeDtypeStruct((t, d), F32)]
    out_specs = [pl.BlockSpec((bm, d), row)]
    if not final:
        out_shape.append(jax.ShapeDtypeStruct((t, d), BF16))
        out_specs.append(pl.BlockSpec((bm, d), row))
    res = pl.pallas_call(
        functools.partial(_ffn_body, final=final),
        grid=grid,
        in_specs=[
            pl.BlockSpec((bm, d), row),
            pl.BlockSpec((1, d), lambda i, f: (0, 0)),
            pl.BlockSpec((d, tf), lambda i, f: (0, f)),
            pl.BlockSpec((d, tf), lambda i, f: (0, f)),
            pl.BlockSpec((tf, d), lambda i, f: (f, 0)),
            pl.BlockSpec((1, d), lambda i, f: (0, 0)),
        ],
        out_specs=out_specs,
        out_shape=out_shape,
        scratch_shapes=[pltpu.VMEM((bm, d), BF16)],
        compiler_params=pltpu.CompilerParams(
            dimension_semantics=("parallel", "arbitrary"), vmem_limit_bytes=VMEM_LIMIT),
        name="ffn_final" if final else "ffn",
    )(x, g, wg, wu, wd, ng)
    return res[0] if final else res


def _mm_body(a_ref, b_ref, o_ref):
    o_ref[...] = jnp.dot(a_ref[...], b_ref[...], preferred_element_type=F32).astype(o_ref.dtype)


def _mm(a, b, out_dtype, *, bm=1024, tn=1024, name):
    m, k = a.shape
    n = b.shape[1]
    return pl.pallas_call(
        _mm_body,
        grid=(m // bm, n // tn),
        in_specs=[pl.BlockSpec((bm, k), lambda i, j: (i, 0)),
                  pl.BlockSpec((k, tn), lambda i, j: (0, j))],
        out_specs=pl.BlockSpec((bm, tn), lambda i, j: (i, j)),
        out_shape=jax.ShapeDtypeStruct((m, n), out_dtype),
        compiler_params=pltpu.CompilerParams(
            dimension_semantics=("parallel", "arbitrary"), vmem_limit_bytes=VMEM_LIMIT),
        name=name,
    )(a, b)


def _mm_nt_body(wt_ref, a_ref, o_ref):
    o_ref[...] = lax.dot_general(wt_ref[...], a_ref[...], (((1,), (1,)), ((), ())),
                                 preferred_element_type=F32).astype(o_ref.dtype)


def _mm_nt(wt, a, out_dtype, *, bm=1024, name):
    n, k = wt.shape
    m = a.shape[0]
    return pl.pallas_call(
        _mm_nt_body,
        grid=(m // bm,),
        in_specs=[pl.BlockSpec((n, k), lambda i: (0, 0)),
                  pl.BlockSpec((bm, k), lambda i: (i, 0))],
        out_specs=pl.BlockSpec((n, bm), lambda i: (0, i)),
        out_shape=jax.ShapeDtypeStruct((n, m), out_dtype),
        compiler_params=pltpu.CompilerParams(
            dimension_semantics=("parallel",), vmem_limit_bytes=VMEM_LIMIT),
        name=name,
    )(wt, a)


def _t5_bias(d, tbl_ref, h):
    n = jnp.maximum(d, 0)
    nf = jnp.maximum(n, 1).astype(F32)
    large = MAX_EXACT + (jnp.log(nf / MAX_EXACT) / math.log(MAX_DISTANCE / MAX_EXACT)
                         * (NUM_BUCKETS - MAX_EXACT)).astype(jnp.int32)
    large = jnp.minimum(large, NUM_BUCKETS - 1)
    bucket = jnp.where(n < MAX_EXACT, n, large)
    b = jnp.zeros(d.shape, F32)
    for j in range(NUM_BUCKETS):
        b = jnp.where(bucket == j, tbl_ref[j, h], b)
    b = b - tbl_ref[NUM_BUCKETS - 1, h]
    return jnp.where(d >= 0, b, NEG_INF)


def _attn_body(tbl_ref, lam_ref, g_ref, q_ref, k_ref, vt_ref, o_ref, bias_ref, *, lam_init, seq):
    h = pl.program_id(1)
    blk, grp = ATT_BLK, ATT_GROUP
    width = grp * blk
    nb = 2 * grp
    kk = lax.broadcasted_iota(jnp.int32, (blk, blk), 0)
    jj = lax.broadcasted_iota(jnp.int32, (blk, blk), 1)
    diag = _t5_bias(jj - kk, tbl_ref, h)
    sub = _t5_bias(blk + jj - kk, tbl_ref, h)
    bias_ref[:(nb - 2) * blk, :] = jnp.zeros(((nb - 2) * blk, 2 * blk), F32)
    bias_ref[(nb - 2) * blk:(nb - 1) * blk, :] = jnp.concatenate([sub, sub], axis=1)
    bias_ref[(nb - 1) * blk:, :] = jnp.concatenate([diag, diag], axis=1)

    lv = lam_ref[...]
    lam = (jnp.exp(jnp.sum(lv[0:1] * lv[1:2], axis=-1, keepdims=True))
           - jnp.exp(jnp.sum(lv[2:3] * lv[3:4], axis=-1, keepdims=True)) + lam_init)
    g = g_ref[...]
    lane = lax.broadcasted_iota(jnp.int32, (blk, HEAD_W), 1)
    scale = HEAD_DIM ** -0.5

    def load_qs(q0):
        out = []
        for r in range(grp):
            q = q_ref[pl.ds(q0 + r * blk, blk), :].astype(F32) * scale
            out.append(jnp.concatenate([jnp.where(lane < HEAD_DIM, q, 0.0),
                                        jnp.where(lane >= HEAD_DIM, q, 0.0)],
                                       axis=0).astype(BF16))
        return out

    def scores(kb, qs, bias):
        st = lax.dot_general(kb, qs, (((1,), (1,)), ((), ())), preferred_element_type=F32)
        return st if bias is None else st + bias

    def update(state, st, vt):
        m, l, acc = state
        m_new = jnp.maximum(m, jnp.max(st, axis=0, keepdims=True))
        alpha = jnp.exp(m - m_new)
        p = jnp.exp(st - m_new)
        l = alpha * l + jnp.sum(p, axis=0, keepdims=True)
        acc = alpha * acc + jnp.dot(vt, p.astype(BF16), preferred_element_type=F32)
        return m_new, l, acc

    def finish(state, q0):
        _, l, acc = state
        o = acc * (1.0 / l)
        ot = o[:, :blk] - lam * o[:, blk:]
        y = ot * lax.rsqrt(jnp.mean(ot * ot, axis=0, keepdims=True) + 1e-5) * g
        y = y * (1.0 - lam_init)
        o_ref[pl.ds(q0, blk), :] = y.T.astype(o_ref.dtype)

    init = (jnp.full((1, 2 * blk), NEG_INF, F32), jnp.zeros((1, 2 * blk), F32),
            jnp.zeros((HEAD_W, 2 * blk), F32))

    def tails(first, q0, qs, states):
        k0 = 0 if first else q0 - width
        span = width if first else 2 * width
        kb_all = k_ref[pl.ds(k0, span), :]
        vt_all = vt_ref[:, pl.ds(k0, span)]
        nts = [r + 1 if first else grp + r + 1 for r in range(grp)]
        sts = [scores(kb_all[:nt * blk], qs[r], bias_ref[(nb - nt) * blk:, :])
               for r, nt in enumerate(nts)]
        for r, nt in enumerate(nts):
            finish(update(states[r], sts[r], vt_all[:, :nt * blk]), q0 + r * blk)

    tails(True, 0, load_qs(0), (init,) * grp)

    def group(gi, carry):
        q0 = pl.multiple_of(gi * width, width)
        qs = load_qs(q0)

        def far(i, states):
            k0 = pl.multiple_of(i * width, width)
            kb = k_ref[pl.ds(k0, width), :]
            vt = vt_ref[:, pl.ds(k0, width)]
            sts = [scores(kb, qs[r], None) for r in range(grp)]
            return tuple(update(states[r], sts[r], vt) for r in range(grp))

        states = lax.fori_loop(0, gi - 1, far, (init,) * grp)
        tails(False, q0, qs, states)
        return carry

    lax.fori_loop(1, seq // width, group, 0)


def _attention(qk, vt, tbl, lam_vecs, g_col, *, batch, seq, lam_init):
    t = qk.shape[0]
    return pl.pallas_call(
        functools.partial(_attn_body, lam_init=lam_init, seq=seq),
        grid=(batch, N_HEADS),
        in_specs=[
            pl.BlockSpec(memory_space=pltpu.SMEM),
            pl.BlockSpec((4, HEAD_DIM), lambda b, h: (0, 0)),
            pl.BlockSpec((HEAD_W, 1), lambda b, h: (0, 0)),
            pl.BlockSpec((seq, HEAD_W), lambda b, h: (b, h)),
            pl.BlockSpec((seq, HEAD_W), lambda b, h: (b, N_HEADS + h)),
            pl.BlockSpec((HEAD_W, seq), lambda b, h: (h, b)),
        ],
        out_specs=pl.BlockSpec((seq, HEAD_W), lambda b, h: (b, h)),
        out_shape=jax.ShapeDtypeStruct((t, N_HEADS * HEAD_W), BF16),
        scratch_shapes=[pltpu.VMEM((2 * ATT_GROUP * ATT_BLK, 2 * ATT_BLK), F32)],
        compiler_params=pltpu.CompilerParams(
            dimension_semantics=("parallel", "parallel"), vmem_limit_bytes=VMEM_LIMIT),
        name="diff_attn",
    )(tbl, lam_vecs, g_col, qk, qk, vt)


def _conv_body(cur_ref, halo_ref, w_ref, b_ref, g_ref, beta_ref, o_ref, ext_ref, *, blocks_per_seq):
    i = pl.program_id(0)
    c = o_ref.shape[1]
    bs = o_ref.shape[0]
    cur = cur_ref[...]
    ext_ref[CONV_HALO:, :] = cur[:, :c] * _sigmoid(cur[:, c:])
    hal = halo_ref[...]
    uh = hal[:, :c] * _sigmoid(hal[:, c:])
    ext_ref[:CONV_HALO, :] = jnp.where(i % blocks_per_seq == 0, 0.0, uh)

    def rows(r, carry):
        r0 = pl.multiple_of(r * CONV_ROWS, CONV_ROWS)
        win = ext_ref[pl.ds(r0, CONV_ROWS + CONV_HALO), :]
        acc = jnp.zeros((CONV_ROWS, c), F32)
        for j in range(CONV_WIDTH):
            off = CONV_HALO - (CONV_WIDTH - 1) + j
            acc = acc + w_ref[j:j + 1, :] * win[off:off + CONV_ROWS, :]
        y = acc + b_ref[...]
        mu = jnp.mean(y, axis=-1, keepdims=True)
        var = jnp.mean(jnp.square(y - mu), axis=-1, keepdims=True)
        yn = (y - mu) * lax.rsqrt(var + 1e-5) * g_ref[...] + beta_ref[...]
        o_ref[pl.ds(r0, CONV_ROWS), :] = (yn * _sigmoid(yn)).astype(o_ref.dtype)
        return carry

    lax.fori_loop(0, bs // CONV_ROWS, rows, 0)


def _conv_branch(glu, w, b, g, beta, *, seq, bs=512):
    t, c2 = glu.shape
    c = c2 // 2
    per = bs // CONV_HALO
    vec = lambda i: (0, 0)
    return pl.pallas_call(
        functools.partial(_conv_body, blocks_per_seq=seq // bs),
        grid=(t // bs,),
        in_specs=[
            pl.BlockSpec((bs, c2), lambda i: (i, 0)),
            pl.BlockSpec((CONV_HALO, c2), lambda i: (jnp.maximum(i * per - 1, 0), 0)),
            pl.BlockSpec((CONV_WIDTH, c), vec),
            pl.BlockSpec((1, c), vec),
            pl.BlockSpec((1, c), vec),
            pl.BlockSpec((1, c), vec),
        ],
        out_specs=pl.BlockSpec((bs, c), lambda i: (i, 0)),
        out_shape=jax.ShapeDtypeStruct((t, c), BF16),
        scratch_shapes=[pltpu.VMEM((CONV_HALO + bs, c), F32)],
        compiler_params=pltpu.CompilerParams(
            dimension_semantics=("parallel",), vmem_limit_bytes=VMEM_LIMIT),
        name="conv_branch",
    )(glu, glu, w, b, g, beta)


def _merge_body(x_ref, a_ref, u_ref, gate_ref, wa_ref, wc_ref, wo_ref, o_ref):
    d = o_ref.shape[1]
    ab = jnp.dot(a_ref[...], wa_ref[...], preferred_element_type=F32)
    cb = jnp.dot(u_ref[...], wc_ref[...], preferred_element_type=F32)
    gates = gate_ref[...]
    merged = _sigmoid(gates[:, :d]) * ab + _sigmoid(gates[:, d:]) * cb
    o_ref[...] = x_ref[...] + jnp.dot(merged.astype(BF16), wo_ref[...],
                                      preferred_element_type=F32)


def _merge(x1, attn_o, conv_u, gates, wa, wc, wo, *, bm=256):
    t, d = x1.shape
    row = lambda i: (i, 0)
    const = lambda i: (0, 0)
    resident = dict(pipeline_mode=pl.Buffered(1))
    return pl.pallas_call(
        _merge_body,
        grid=(t // bm,),
        in_specs=[
            pl.BlockSpec((bm, d), row),
            pl.BlockSpec((bm, attn_o.shape[1]), row),
            pl.BlockSpec((bm, conv_u.shape[1]), row),
            pl.BlockSpec((bm, 2 * d), row),
            pl.BlockSpec(wa.shape, const, **resident),
            pl.BlockSpec(wc.shape, const, **resident),
            pl.BlockSpec(wo.shape, const, **resident),
        ],
        out_specs=pl.BlockSpec((bm, d), row),
        out_shape=jax.ShapeDtypeStruct((t, d), F32),
        compiler_params=pltpu.CompilerParams(
            dimension_semantics=("parallel",), vmem_limit_bytes=VMEM_LIMIT),
        name="merge",
    )(x1, attn_o, conv_u, gates, wa, wc, wo)


def kernel(x, ffn1_norm_g, ffn1_w_gate, ffn1_w_up, ffn1_w_down, mix_norm_g, w_in, rel_bias_table, lambda_q1, lambda_k1, lambda_q2, lambda_k2, attn_head_norm_g, w_attn_branch, conv_dw_w, conv_dw_b, conv_ln_g, conv_ln_b, w_conv_branch, w_out, ffn2_norm_g, ffn2_w_gate, ffn2_w_up, ffn2_w_down, final_norm_g):
    batch, seq, d = x.shape
    depth = ffn1_w_gate.shape[0]
    attn_w = N_HEADS * HEAD_W
    conv_ch = conv_dw_w.shape[2]
    xt = x.reshape(batch * seq, d)
    row = lambda v: v.reshape(1, -1).astype(F32)

    for l in range(depth):
        lam_init = 0.8 - 0.6 * math.exp(-0.3 * l)
        last = l == depth - 1
        w_in_l = w_in[l]
        w_qk = w_in_l[:, :2 * attn_w].astype(BF16)
        w_vt = w_in_l[:, 2 * attn_w:3 * attn_w].T.astype(BF16)
        w_glu = w_in_l[:, 3 * attn_w:3 * attn_w + 2 * conv_ch].astype(BF16)
        w_gates = w_in_l[:, 3 * attn_w + 2 * conv_ch:].astype(BF16)

        x1, h2 = _ffn(xt, row(ffn1_norm_g[l]), ffn1_w_gate[l].astype(BF16),
                      ffn1_w_up[l].astype(BF16), ffn1_w_down[l].astype(BF16),
                      row(mix_norm_g[l]), final=False)

        qk = _mm(h2, w_qk, BF16, name="proj_qk")
        vt = _mm_nt(w_vt, h2, BF16, name="proj_vt")
        glu = _mm(h2, w_glu, F32, name="proj_glu")
        gates = _mm(h2, w_gates, F32, name="proj_gates")

        lam_vecs = jnp.stack([lambda_q1[l], lambda_k1[l], lambda_q2[l], lambda_k2[l]]).astype(F32)
        attn_o = _attention(qk, vt, rel_bias_table.astype(F32), lam_vecs,
                            attn_head_norm_g[l].reshape(HEAD_W, 1).astype(F32),
                            batch=batch, seq=seq, lam_init=lam_init)
        conv_u = _conv_branch(glu, conv_dw_w[l].astype(F32), row(conv_dw_b[l]),
                              row(conv_ln_g[l]), row(conv_ln_b[l]), seq=seq)
        x2 = _merge(x1, attn_o, conv_u, gates, w_attn_branch[l].astype(BF16),
                    w_conv_branch[l].astype(BF16), w_out[l].astype(BF16))

        res = _ffn(x2, row(ffn2_norm_g[l]), ffn2_w_gate[l].astype(BF16),
                   ffn2_w_up[l].astype(BF16), ffn2_w_down[l].astype(BF16),
                   row(final_norm_g), final=last)
        xt = res if last else res[0]

    return xt.reshape(batch, seq, d)
```

```python
import functools
import math

import jax
import jax.numpy as jnp
from jax import lax
from jax.experimental import pallas as pl
from jax.experimental.pallas import tpu as pltpu

F32 = jnp.float32
BF16 = jnp.bfloat16

N_HEADS = 8
HEAD_DIM = 64
HEAD_W = 2 * HEAD_DIM
CONV_WIDTH = 31
NUM_BUCKETS = 32
MAX_EXACT = NUM_BUCKETS // 2
MAX_DISTANCE = 128
NEG_INF = -1e30
LANES = 128
SUBLANES = 8
MIB = 1024 * 1024
VMEM_LIMIT = 56 * MIB

ATT_BLK = 128
ATT_GROUP = 4
CONV_HALO = 32
CONV_ROWS = 32
CONV_SHIFT_ROWS = 64


def _rms(x, g, eps):
    return x * lax.rsqrt(jnp.mean(x * x, axis=-1, keepdims=True) + eps) * g


def _sigmoid(x):
    return 1.0 / (1.0 + jnp.exp(-x))


def _ffn_body(x_ref, g_ref, wg_ref, wu_ref, wd_ref, ng_ref, *rest, final):
    if final:
        o_ref, h_ref = rest
    else:
        o_ref, hn_ref, h_ref = rest
    f = pl.program_id(1)

    @pl.when(f == 0)
    def _():
        x = x_ref[...]
        h_ref[...] = _rms(x, g_ref[...], 1e-6).astype(BF16)
        o_ref[...] = x

    h = h_ref[...]
    gate = jnp.dot(h, wg_ref[...], preferred_element_type=F32)
    up = jnp.dot(h, wu_ref[...], preferred_element_type=F32)
    act = (gate * _sigmoid(gate)) * up * 0.5
    o_ref[...] += jnp.dot(act.astype(BF16), wd_ref[...], preferred_element_type=F32)

    @pl.when(f == pl.num_programs(1) - 1)
    def _():
        y = _rms(o_ref[...], ng_ref[...], 1e-6)
        if final:
            o_ref[...] = y
        else:
            hn_ref[...] = y.astype(BF16)


def _ffn(x, g, wg, wu, wd, ng, *, final, bm=512, tf=512):
    t, d = x.shape
    dff = wg.shape[1]
    grid = (t // bm, dff // tf)
    row = lambda i, f: (i, 0)
    out_shape = [jax.ShapeDtypeStruct((t, d), F32)]
    out_specs = [pl.BlockSpec((bm, d), row)]
    if not final:
        out_shape.append(jax.ShapeDtypeStruct((t, d), BF16))
        out_specs.append(pl.BlockSpec((bm, d), row))
    res = pl.pallas_call(
        functools.partial(_ffn_body, final=final),
        grid=grid,
        in_specs=[
            pl.BlockSpec((bm, d), row),
            pl.BlockSpec((1, d), lambda i, f: (0, 0)),
            pl.BlockSpec((d, tf), lambda i, f: (0, f)),
            pl.BlockSpec((d, tf), lambda i, f: (0, f)),
            pl.BlockSpec((tf, d), lambda i, f: (f, 0)),
            pl.BlockSpec((1, d), lambda i, f: (0, 0)),
        ],
        out_specs=out_specs,
        out_shape=out_shape,
        scratch_shapes=[pltpu.VMEM((bm, d), BF16)],
        compiler_params=pltpu.CompilerParams(
            dimension_semantics=("parallel", "arbitrary"), vmem_limit_bytes=VMEM_LIMIT),
        name="ffn_final" if final else "ffn",
    )(x, g, wg, wu, wd, ng)
    return res[0] if final else res


def _mm_body(a_ref, b_ref, o_ref):
    o_ref[...] = jnp.dot(a_ref[...], b_ref[...], preferred_element_type=F32).astype(o_ref.dtype)


def _mm(a, b, out_dtype, *, bm=1024, tn=1024, name):
    m, k = a.shape
    n = b.shape[1]
    return pl.pallas_call(
        _mm_body,
        grid=(m // bm, n // tn),
        in_specs=[pl.BlockSpec((bm, k), lambda i, j: (i, 0)),
                  pl.BlockSpec((k, tn), lambda i, j: (0, j))],
        out_specs=pl.BlockSpec((bm, tn), lambda i, j: (i, j)),
        out_shape=jax.ShapeDtypeStruct((m, n), out_dtype),
        compiler_params=pltpu.CompilerParams(
            dimension_semantics=("parallel", "arbitrary"), vmem_limit_bytes=VMEM_LIMIT),
        name=name,
    )(a, b)


def _mm_nt_body(wt_ref, a_ref, o_ref):
    o_ref[...] = lax.dot_general(wt_ref[...], a_ref[...], (((1,), (1,)), ((), ())),
                                 preferred_element_type=F32).astype(o_ref.dtype)


def _mm_nt(wt, a, out_dtype, *, bm=1024, name):
    n, k = wt.shape
    m = a.shape[0]
    return pl.pallas_call(
        _mm_nt_body,
        grid=(m // bm,),
        in_specs=[pl.BlockSpec((n, k), lambda i: (0, 0)),
                  pl.BlockSpec((bm, k), lambda i: (i, 0))],
        out_specs=pl.BlockSpec((n, bm), lambda i: (0, i)),
        out_shape=jax.ShapeDtypeStruct((n, m), out_dtype),
        compiler_params=pltpu.CompilerParams(
            dimension_semantics=("parallel",), vmem_limit_bytes=VMEM_LIMIT),
        name=name,
    )(wt, a)


def _t5_bias(d, tbl_ref, h):
    n = jnp.maximum(d, 0)
    nf = jnp.maximum(n, 1).astype(F32)
    large = MAX_EXACT + (jnp.log(nf / MAX_EXACT) / math.log(MAX_DISTANCE / MAX_EXACT)
                         * (NUM_BUCKETS - MAX_EXACT)).astype(jnp.int32)
    large = jnp.minimum(large, NUM_BUCKETS - 1)
    bucket = jnp.where(n < MAX_EXACT, n, large)
    b = jnp.zeros(d.shape, F32)
    for j in range(NUM_BUCKETS):
        b = jnp.where(bucket == j, tbl_ref[j, h], b)
    b = b - tbl_ref[NUM_BUCKETS - 1, h]
    return jnp.where(d >= 0, b, NEG_INF)


def _attn_body(tbl_ref, lam_ref, g_ref, q_ref, k_ref, vt_ref, o_ref, bias_ref, *, lam_init, seq):
    h = pl.program_id(1)
    blk, grp = ATT_BLK, ATT_GROUP
    width = grp * blk
    nb = 2 * grp
    kk = lax.broadcasted_iota(jnp.int32, (blk, blk), 0)
    jj = lax.broadcasted_iota(jnp.int32, (blk, blk), 1)
    diag = _t5_bias(jj - kk, tbl_ref, h)
    sub = _t5_bias(blk + jj - kk, tbl_ref, h)
    bias_ref[:(nb - 2) * blk, :] = jnp.zeros(((nb - 2) * blk, 2 * blk), F32)
    bias_ref[(nb - 2) * blk:(nb - 1) * blk, :] = jnp.concatenate([sub, sub], axis=1)
    bias_ref[(nb - 1) * blk:, :] = jnp.concatenate([diag, diag], axis=1)

    lv = lam_ref[...]
    lam = (jnp.exp(jnp.sum(lv[0:1] * lv[1:2], axis=-1, keepdims=True))
           - jnp.exp(jnp.sum(lv[2:3] * lv[3:4], axis=-1, keepdims=True)) + lam_init)
    g = g_ref[...]
    lane = lax.broadcasted_iota(jnp.int32, (blk, HEAD_W), 1)
    scale = HEAD_DIM ** -0.5

    def load_qs(q0):
        out = []
        for r in range(grp):
            q = q_ref[pl.ds(q0 + r * blk, blk), :].astype(F32) * scale
            out.append(jnp.concatenate([jnp.where(lane < HEAD_DIM, q, 0.0),
                                        jnp.where(lane >= HEAD_DIM, q, 0.0)],
                                       axis=0).astype(BF16))
        return out

    def scores(kb, qs, bias):
        st = lax.dot_general(kb, qs, (((1,), (1,)), ((), ())), preferred_element_type=F32)
        return st if bias is None else st + bias

    def update(state, st, vt):
        m, l, acc = state
        m_new = jnp.maximum(m, jnp.max(st, axis=0, keepdims=True))
        alpha = jnp.exp(m - m_new)
        p = jnp.exp(st - m_new)
        l = alpha * l + jnp.sum(p, axis=0, keepdims=True)
        acc = alpha * acc + jnp.dot(vt, p.astype(BF16), preferred_element_type=F32)
        return m_new, l, acc

    def finish(state, q0):
        _, l, acc = state
        o = acc * (1.0 / l)
        ot = o[:, :blk] - lam * o[:, blk:]
        y = ot * lax.rsqrt(jnp.mean(ot * ot, axis=0, keepdims=True) + 1e-5) * g
        y = y * (1.0 - lam_init)
        o_ref[pl.ds(q0, blk), :] = y.T.astype(o_ref.dtype)

    init = (jnp.full((1, 2 * blk), NEG_INF, F32), jnp.zeros((1, 2 * blk), F32),
            jnp.zeros((HEAD_W, 2 * blk), F32))

    def tails(first, q0, qs, states):
        k0 = 0 if first else q0 - width
        span = width if first else 2 * width
        kb_all = k_ref[pl.ds(k0, span), :]
        vt_all = vt_ref[:, pl.ds(k0, span)]
        nts = [r + 1 if first else grp + r + 1 for r in range(grp)]
        sts = [scores(kb_all[:nt * blk], qs[r], bias_ref[(nb - nt) * blk:, :])
               for r, nt in enumerate(nts)]
        for r, nt in enumerate(nts):
            finish(update(states[r], sts[r], vt_all[:, :nt * blk]), q0 + r * blk)

    tails(True, 0, load_qs(0), (init,) * grp)

    def group(gi, carry):
        q0 = pl.multiple_of(gi * width, width)
        qs = load_qs(q0)

        def far(i, states):
            k0 = pl.multiple_of(i * width, width)
            kb = k_ref[pl.ds(k0, width), :]
            vt = vt_ref[:, pl.ds(k0, width)]
            sts = [scores(kb, qs[r], None) for r in range(grp)]
            return tuple(update(states[r], sts[r], vt) for r in range(grp))

        states = lax.fori_loop(0, gi - 1, far, (init,) * grp)
        tails(False, q0, qs, states)
        return carry

    lax.fori_loop(1, seq // width, group, 0)


def _attention(qk, vt, tbl, lam_vecs, g_col, *, batch, seq, lam_init):
    t = qk.shape[0]
    return pl.pallas_call(
        functools.partial(_attn_body, lam_init=lam_init, seq=seq),
        grid=(batch, N_HEADS),
        in_specs=[
            pl.BlockSpec(memory_space=pltpu.SMEM),
            pl.BlockSpec((4, HEAD_DIM), lambda b, h: (0, 0)),
            pl.BlockSpec((HEAD_W, 1), lambda b, h: (0, 0)),
            pl.BlockSpec((seq, HEAD_W), lambda b, h: (b, h)),
            pl.BlockSpec((seq, HEAD_W), lambda b, h: (b, N_HEADS + h)),
            pl.BlockSpec((HEAD_W, seq), lambda b, h: (h, b)),
        ],
        out_specs=pl.BlockSpec((seq, HEAD_W), lambda b, h: (b, h)),
        out_shape=jax.ShapeDtypeStruct((t, N_HEADS * HEAD_W), BF16),
        scratch_shapes=[pltpu.VMEM((2 * ATT_GROUP * ATT_BLK, 2 * ATT_BLK), F32)],
        compiler_params=pltpu.CompilerParams(
            dimension_semantics=("parallel", "parallel"), vmem_limit_bytes=VMEM_LIMIT),
        name="diff_attn",
    )(tbl, lam_vecs, g_col, qk, qk, vt)


def _conv_body(cur_ref, halo_ref, w_ref, b_ref, g_ref, beta_ref, o_ref, ext_ref, *, blocks_per_seq):
    i = pl.program_id(0)
    c = o_ref.shape[1]
    bs = o_ref.shape[0]
    cur = cur_ref[...]
    ext_ref[0, CONV_HALO:, :] = cur[:, :c] * _sigmoid(cur[:, c:])
    hal = halo_ref[...]
    uh = hal[:, :c] * _sigmoid(hal[:, c:])
    ext_ref[0, :CONV_HALO, :] = jnp.where(i % blocks_per_seq == 0, 0.0, uh)

    def shift_rows(r0, n):
        win = ext_ref[0, pl.ds(r0, n + SUBLANES), :]
        for s in range(1, SUBLANES):
            ext_ref[s, pl.ds(r0, n), :] = win[s:s + n]

    def shift(ci, carry):
        shift_rows(pl.multiple_of(ci * CONV_SHIFT_ROWS, CONV_SHIFT_ROWS), CONV_SHIFT_ROWS)
        return carry

    lax.fori_loop(0, bs // CONV_SHIFT_ROWS, shift, 0)
    shift_rows(bs, CONV_HALO - SUBLANES)

    groups = CONV_ROWS // SUBLANES

    def conv_rows(r):
        r0 = pl.multiple_of(r * CONV_ROWS, CONV_ROWS)
        accs = [jnp.zeros((SUBLANES, c), F32) for _ in range(groups)]
        for j in range(CONV_WIDTH):
            off = CONV_HALO - (CONV_WIDTH - 1) + j
            s, a = off % SUBLANES, off // SUBLANES
            w_j = w_ref[j * SUBLANES:(j + 1) * SUBLANES, :]
            for k in range(groups):
                start = pl.multiple_of(r0 + (a + k) * SUBLANES, SUBLANES)
                accs[k] = accs[k] + w_j * ext_ref[s, pl.ds(start, SUBLANES), :]
        return jnp.concatenate(accs, axis=0) + b_ref[...]

    def norm_rows(r, y):
        mu = jnp.mean(y, axis=-1, keepdims=True)
        var = jnp.mean(jnp.square(y - mu), axis=-1, keepdims=True)
        yn = (y - mu) * lax.rsqrt(var + 1e-5) * g_ref[...] + beta_ref[...]
        r0 = pl.multiple_of(r * CONV_ROWS, CONV_ROWS)
        o_ref[pl.ds(r0, CONV_ROWS), :] = (yn * _sigmoid(yn)).astype(o_ref.dtype)

    def rows(r, y_prev):
        y = conv_rows(r)
        norm_rows(r - 1, y_prev)
        return y

    n_chunks = bs // CONV_ROWS
    norm_rows(n_chunks - 1, lax.fori_loop(1, n_chunks, rows, conv_rows(0)))


def _conv_branch(glu, w, b, g, beta, *, seq, bs=512):
    t, c2 = glu.shape
    c = c2 // 2
    per = bs // CONV_HALO
    vec = lambda i: (0, 0)
    return pl.pallas_call(
        functools.partial(_conv_body, blocks_per_seq=seq // bs),
        grid=(t // bs,),
        in_specs=[
            pl.BlockSpec((bs, c2), lambda i: (i, 0)),
            pl.BlockSpec((CONV_HALO, c2), lambda i: (jnp.maximum(i * per - 1, 0), 0)),
            pl.BlockSpec((CONV_WIDTH * SUBLANES, c), vec),
            pl.BlockSpec((1, c), vec),
            pl.BlockSpec((1, c), vec),
            pl.BlockSpec((1, c), vec),
        ],
        out_specs=pl.BlockSpec((bs, c), lambda i: (i, 0)),
        out_shape=jax.ShapeDtypeStruct((t, c), BF16),
        scratch_shapes=[pltpu.VMEM((SUBLANES, CONV_HALO + bs, c), F32)],
        compiler_params=pltpu.CompilerParams(
            dimension_semantics=("parallel",), vmem_limit_bytes=VMEM_LIMIT),
        name="conv_branch",
    )(glu, glu, w, b, g, beta)


def _merge_body(x_ref, a_ref, u_ref, gate_ref, wa_ref, wc_ref, wo_ref, o_ref):
    d = o_ref.shape[1]
    ab = jnp.dot(a_ref[...], wa_ref[...], preferred_element_type=F32)
    cb = jnp.dot(u_ref[...], wc_ref[...], preferred_element_type=F32)
    gates = gate_ref[...]
    merged = _sigmoid(gates[:, :d]) * ab + _sigmoid(gates[:, d:]) * cb
    o_ref[...] = x_ref[...] + jnp.dot(merged.astype(BF16), wo_ref[...],
                                      preferred_element_type=F32)


def _merge(x1, attn_o, conv_u, gates, wa, wc, wo, *, bm=256):
    t, d = x1.shape
    row = lambda i: (i, 0)
    const = lambda i: (0, 0)
    resident = dict(pipeline_mode=pl.Buffered(1))
    return pl.pallas_call(
        _merge_body,
        grid=(t // bm,),
        in_specs=[
            pl.BlockSpec((bm, d), row),
            pl.BlockSpec((bm, attn_o.shape[1]), row),
            pl.BlockSpec((bm, conv_u.shape[1]), row),
            pl.BlockSpec((bm, 2 * d), row),
            pl.BlockSpec(wa.shape, const, **resident),
            pl.BlockSpec(wc.shape, const, **resident),
            pl.BlockSpec(wo.shape, const, **resident),
        ],
        out_specs=pl.BlockSpec((bm, d), row),
        out_shape=jax.ShapeDtypeStruct((t, d), F32),
        compiler_params=pltpu.CompilerParams(
            dimension_semantics=("parallel",), vmem_limit_bytes=VMEM_LIMIT),
        name="merge",
    )(x1, attn_o, conv_u, gates, wa, wc, wo)


def kernel(x, ffn1_norm_g, ffn1_w_gate, ffn1_w_up, ffn1_w_down, mix_norm_g, w_in, rel_bias_table, lambda_q1, lambda_k1, lambda_q2, lambda_k2, attn_head_norm_g, w_attn_branch, conv_dw_w, conv_dw_b, conv_ln_g, conv_ln_b, w_conv_branch, w_out, ffn2_norm_g, ffn2_w_gate, ffn2_w_up, ffn2_w_down, final_norm_g):
    batch, seq, d = x.shape
    depth = ffn1_w_gate.shape[0]
    attn_w = N_HEADS * HEAD_W
    conv_ch = conv_dw_w.shape[2]
    xt = x.reshape(batch * seq, d)
    row = lambda v: v.reshape(1, -1).astype(F32)

    for l in range(depth):
        lam_init = 0.8 - 0.6 * math.exp(-0.3 * l)
        last = l == depth - 1
        w_in_l = w_in[l]
        w_qk = w_in_l[:, :2 * attn_w].astype(BF16)
        w_vt = w_in_l[:, 2 * attn_w:3 * attn_w].T.astype(BF16)
        w_glu = w_in_l[:, 3 * attn_w:3 * attn_w + 2 * conv_ch].astype(BF16)
        w_gates = w_in_l[:, 3 * attn_w + 2 * conv_ch:].astype(BF16)

        x1, h2 = _ffn(xt, row(ffn1_norm_g[l]), ffn1_w_gate[l].astype(BF16),
                      ffn1_w_up[l].astype(BF16), ffn1_w_down[l].astype(BF16),
                      row(mix_norm_g[l]), final=False)

        qk = _mm(h2, w_qk, BF16, name="proj_qk")
        vt = _mm_nt(w_vt, h2, BF16, name="proj_vt")
        glu = _mm(h2, w_glu, F32, name="proj_glu")
        gates = _mm(h2, w_gates, F32, name="proj_gates")

        lam_vecs = jnp.stack([lambda_q1[l], lambda_k1[l], lambda_q2[l], lambda_k2[l]]).astype(F32)
        attn_o = _attention(qk, vt, rel_bias_table.astype(F32), lam_vecs,
                            attn_head_norm_g[l].reshape(HEAD_W, 1).astype(F32),
                            batch=batch, seq=seq, lam_init=lam_init)
        conv_w = jnp.repeat(conv_dw_w[l].astype(F32), SUBLANES, axis=0)
        conv_u = _conv_branch(glu, conv_w, row(conv_dw_b[l]),
                              row(conv_ln_g[l]), row(conv_ln_b[l]), seq=seq)
        x2 = _merge(x1, attn_o, conv_u, gates, w_attn_branch[l].astype(BF16),
                    w_conv_branch[l].astype(BF16), w_out[l].astype(BF16))

        res = _ffn(x2, row(ffn2_norm_g[l]), ffn2_w_gate[l].astype(BF16),
                   ffn2_w_up[l].astype(BF16), ffn2_w_down[l].astype(BF16),
                   row(final_norm_g), final=last)
        xt = res if last else res[0]

    return xt.reshape(batch, seq, d)
```

```python
import functools
import math

import jax
import jax.numpy as jnp
from jax import lax
from jax.experimental import pallas as pl
from jax.experimental.pallas import tpu as pltpu

F32 = jnp.float32
BF16 = jnp.bfloat16

N_HEADS = 8
HEAD_DIM = 64
HEAD_W = 2 * HEAD_DIM
CONV_WIDTH = 31
NUM_BUCKETS = 32
MAX_EXACT = NUM_BUCKETS // 2
MAX_DISTANCE = 128
NEG_INF = -1e30
LANES = 128
SUBLANES = 8
MIB = 1024 * 1024
VMEM_LIMIT = 56 * MIB

ATT_BLK = 128
ATT_GROUP = 4
CONV_HALO = 32
CONV_ROWS = 32
CONV_SHIFT_ROWS = 64


def _rms(x, g, eps):
    return x * lax.rsqrt(jnp.mean(x * x, axis=-1, keepdims=True) + eps) * g


def _sigmoid(x):
    return 1.0 / (1.0 + jnp.exp(-x))


def _ffn_body(x_ref, g_ref, wg_ref, wu_ref, wd_ref, ng_ref, *rest, final):
    if final:
        o_ref, h_ref = rest
    else:
        o_ref, hn_ref, h_ref = rest
    f = pl.program_id(1)

    @pl.when(f == 0)
    def _():
        x = x_ref[...]
        h_ref[...] = _rms(x, g_ref[...], 1e-6).astype(BF16)
        o_ref[...] = x

    h = h_ref[...]
    gate = jnp.dot(h, wg_ref[...], preferred_element_type=F32)
    up = jnp.dot(h, wu_ref[...], preferred_element_type=F32)
    act = (gate * _sigmoid(gate)) * up * 0.5
    o_ref[...] += jnp.dot(act.astype(BF16), wd_ref[...], preferred_element_type=F32)

    @pl.when(f == pl.num_programs(1) - 1)
    def _():
        y = _rms(o_ref[...], ng_ref[...], 1e-6)
        if final:
            o_ref[...] = y
        else:
            hn_ref[...] = y.astype(BF16)


def _ffn(x, g, wg, wu, wd, ng, *, final, bm=512, tf=512):
    t, d = x.shape
    dff = wg.shape[1]
    grid = (t // bm, dff // tf)
    row = lambda i, f: (i, 0)
    out_shape = [jax.ShapeDtypeStruct((t, d), F32)]
    out_specs = [pl.BlockSpec((bm, d), row)]
    if not final:
        out_shape.append(jax.ShapeDtypeStruct((t, d), BF16))
        out_specs.append(pl.BlockSpec((bm, d), row))
    res = pl.pallas_call(
        functools.partial(_ffn_body, final=final),
        grid=grid,
        in_specs=[
            pl.BlockSpec((bm, d), row),
            pl.BlockSpec((1, d), lambda i, f: (0, 0)),
            pl.BlockSpec((d, tf), lambda i, f: (0, f)),
            pl.BlockSpec((d, tf), lambda i, f: (0, f)),
            pl.BlockSpec((tf, d), lambda i, f: (f, 0)),
            pl.BlockSpec((1, d), lambda i, f: (0, 0)),
        ],
        out_specs=out_specs,
        out_shape=out_shape,
        scratch_shapes=[pltpu.VMEM((bm, d), BF16)],
        compiler_params=pltpu.CompilerParams(
            dimension_semantics=("parallel", "arbitrary"), vmem_limit_bytes=VMEM_LIMIT),
        name="ffn_final" if final else "ffn",
    )(x, g, wg, wu, wd, ng)
    return res[0] if final else res


def _mm_body(a_ref, b_ref, o_ref):
    o_ref[...] = jnp.dot(a_ref[...], b_ref[...], preferred_element_type=F32).astype(o_ref.dtype)


def _mm(a, b, out_dtype, *, bm=1024, tn=1024, name):
    m, k = a.shape
    n = b.shape[1]
    return pl.pallas_call(
        _mm_body,
        grid=(m // bm, n // tn),
        in_specs=[pl.BlockSpec((bm, k), lambda i, j: (i, 0)),
                  pl.BlockSpec((k, tn), lambda i, j: (0, j))],
        out_specs=pl.BlockSpec((bm, tn), lambda i, j: (i, j)),
        out_shape=jax.ShapeDtypeStruct((m, n), out_dtype),
        compiler_params=pltpu.CompilerParams(
            dimension_semantics=("parallel", "arbitrary"), vmem_limit_bytes=VMEM_LIMIT),
        name=name,
    )(a, b)


def _mm_nt_body(wt_ref, a_ref, o_ref):
    o_ref[...] = lax.dot_general(wt_ref[...], a_ref[...], (((1,), (1,)), ((), ())),
                                 preferred_element_type=F32).astype(o_ref.dtype)


def _mm_nt(wt, a, out_dtype, *, bm=1024, name):
    n, k = wt.shape
    m = a.shape[0]
    return pl.pallas_call(
        _mm_nt_body,
        grid=(m // bm,),
        in_specs=[pl.BlockSpec((n, k), lambda i: (0, 0)),
                  pl.BlockSpec((bm, k), lambda i: (i, 0))],
        out_specs=pl.BlockSpec((n, bm), lambda i: (0, i)),
        out_shape=jax.ShapeDtypeStruct((n, m), out_dtype),
        compiler_params=pltpu.CompilerParams(
            dimension_semantics=("parallel",), vmem_limit_bytes=VMEM_LIMIT),
        name=name,
    )(wt, a)


def _t5_bias(d, tbl_ref, h):
    n = jnp.maximum(d, 0)
    nf = jnp.maximum(n, 1).astype(F32)
    large = MAX_EXACT + (jnp.log(nf / MAX_EXACT) / math.log(MAX_DISTANCE / MAX_EXACT)
                         * (NUM_BUCKETS - MAX_EXACT)).astype(jnp.int32)
    large = jnp.minimum(large, NUM_BUCKETS - 1)
    bucket = jnp.where(n < MAX_EXACT, n, large)
    b = jnp.zeros(d.shape, F32)
    for j in range(NUM_BUCKETS):
        b = jnp.where(bucket == j, tbl_ref[j, h], b)
    b = b - tbl_ref[NUM_BUCKETS - 1, h]
    return jnp.where(d >= 0, b, NEG_INF)


def _attn_body(tbl_ref, lam_ref, g_ref, q_ref, k_ref, vt_ref, o_ref, bias_ref, *, lam_init, seq):
    h = pl.program_id(1)
    blk, grp = ATT_BLK, ATT_GROUP
    width = grp * blk
    nb = grp + 1
    kk = lax.broadcasted_iota(jnp.int32, (blk, blk), 0)
    jj = lax.broadcasted_iota(jnp.int32, (blk, blk), 1)
    diag = _t5_bias(jj - kk, tbl_ref, h)
    sub = _t5_bias(blk + jj - kk, tbl_ref, h)
    bias_ref[:(nb - 2) * blk, :] = jnp.zeros(((nb - 2) * blk, 2 * blk), F32)
    bias_ref[(nb - 2) * blk:(nb - 1) * blk, :] = jnp.concatenate([sub, sub], axis=1)
    bias_ref[(nb - 1) * blk:, :] = jnp.concatenate([diag, diag], axis=1)

    lv = lam_ref[...]
    lam = (jnp.exp(jnp.sum(lv[0:1] * lv[1:2], axis=-1, keepdims=True))
           - jnp.exp(jnp.sum(lv[2:3] * lv[3:4], axis=-1, keepdims=True)) + lam_init)
    g = g_ref[...]
    lane = lax.broadcasted_iota(jnp.int32, (blk, HEAD_W), 1)
    scale = HEAD_DIM ** -0.5

    def load_qs(q0):
        out = []
        for r in range(grp):
            q = q_ref[pl.ds(q0 + r * blk, blk), :].astype(F32) * scale
            out.append(jnp.concatenate([jnp.where(lane < HEAD_DIM, q, 0.0),
                                        jnp.where(lane >= HEAD_DIM, q, 0.0)],
                                       axis=0).astype(BF16))
        return out

    def scores(kb, qs, bias):
        st = lax.dot_general(kb, qs, (((1,), (1,)), ((), ())), preferred_element_type=F32)
        return st if bias is None else st + bias

    def update(state, st, vt):
        m, l, acc = state
        m_new = jnp.maximum(m, jnp.max(st, axis=0, keepdims=True))
        alpha = jnp.exp(m - m_new)
        p = jnp.exp(st - m_new)
        l = alpha * l + jnp.sum(p, axis=0, keepdims=True)
        acc = alpha * acc + jnp.dot(vt, p.astype(BF16), preferred_element_type=F32)
        return m_new, l, acc

    def finish(state, q0):
        _, l, acc = state
        o = acc * (1.0 / l)
        ot = o[:, :blk] - lam * o[:, blk:]
        y = ot * lax.rsqrt(jnp.mean(ot * ot, axis=0, keepdims=True) + 1e-5) * g
        y = y * (1.0 - lam_init)
        o_ref[pl.ds(q0, blk), :] = y.T.astype(o_ref.dtype)

    init = ((jnp.full((1, 2 * blk), NEG_INF, F32), jnp.zeros((1, 2 * blk), F32),
             jnp.zeros((HEAD_W, 2 * blk), F32)),) * grp

    def far_scores(qs, k0):
        kb = k_ref[pl.ds(k0, width), :]
        return tuple(scores(kb, qs[r], None) for r in range(grp))

    def prev_scores(qs, q0):
        kb = k_ref[pl.ds(q0 - width, width), :]
        return tuple(scores(kb, qs[r], bias_ref[:width, :] if r == 0 else None)
                     for r in range(grp))

    def own_scores(qs, q0):
        kb = k_ref[pl.ds(q0, width), :]
        return tuple(scores(kb[:(r + 1) * blk], qs[r], bias_ref[(grp - r) * blk:, :])
                     for r in range(grp))

    def updates(states, sts, k0):
        vt = vt_ref[:, pl.ds(k0, width)]
        return tuple(update(states[r], sts[r], vt[:, :sts[r].shape[0]]) for r in range(grp))

    def finish_all(states, q0):
        for r in range(grp):
            finish(states[r], q0 + r * blk)

    steps = []
    for gi in range(seq // width):
        q0 = gi * width
        kinds = ["far"] * max(gi - 1, 0) + (["prev"] if gi else []) + ["own"]
        for n, kind in enumerate(kinds):
            steps.append((gi, kind, n * width, n == len(kinds) - 1))

    def step_scores(step, qs):
        gi, kind, k0, _ = step
        if kind == "far":
            return far_scores(qs, k0)
        return prev_scores(qs, gi * width) if kind == "prev" else own_scores(qs, gi * width)

    qs = load_qs(0)
    cur = step_scores(steps[0], qs)
    states = init
    for n, step in enumerate(steps):
        gi, _, k0, last = step
        nxt = None
        if n + 1 < len(steps):
            if last:
                qs = load_qs(steps[n + 1][0] * width)
            nxt = step_scores(steps[n + 1], qs)
        states = updates(states, cur, k0)
        if last:
            finish_all(states, gi * width)
            states = init
        cur = nxt


def _attention(qk, vt, tbl, lam_vecs, g_col, *, batch, seq, lam_init):
    t = qk.shape[0]
    return pl.pallas_call(
        functools.partial(_attn_body, lam_init=lam_init, seq=seq),
        grid=(batch, N_HEADS),
        in_specs=[
            pl.BlockSpec(memory_space=pltpu.SMEM),
            pl.BlockSpec((4, HEAD_DIM), lambda b, h: (0, 0)),
            pl.BlockSpec((HEAD_W, 1), lambda b, h: (0, 0)),
            pl.BlockSpec((seq, HEAD_W), lambda b, h: (b, h)),
            pl.BlockSpec((seq, HEAD_W), lambda b, h: (b, N_HEADS + h)),
            pl.BlockSpec((HEAD_W, seq), lambda b, h: (h, b)),
        ],
        out_specs=pl.BlockSpec((seq, HEAD_W), lambda b, h: (b, h)),
        out_shape=jax.ShapeDtypeStruct((t, N_HEADS * HEAD_W), BF16),
        scratch_shapes=[pltpu.VMEM(((ATT_GROUP + 1) * ATT_BLK, 2 * ATT_BLK), F32)],
        compiler_params=pltpu.CompilerParams(
            dimension_semantics=("parallel", "parallel"), vmem_limit_bytes=VMEM_LIMIT),
        name="diff_attn",
    )(tbl, lam_vecs, g_col, qk, qk, vt)


def _conv_body(cur_ref, halo_ref, w_ref, b_ref, g_ref, beta_ref, o_ref, ext_ref, *, blocks_per_seq):
    i = pl.program_id(0)
    c = o_ref.shape[1]
    bs = o_ref.shape[0]
    cur = cur_ref[...]
    ext_ref[0, CONV_HALO:, :] = cur[:, :c] * _sigmoid(cur[:, c:])
    hal = halo_ref[...]
    uh = hal[:, :c] * _sigmoid(hal[:, c:])
    ext_ref[0, :CONV_HALO, :] = jnp.where(i % blocks_per_seq == 0, 0.0, uh)

    def shift_rows(r0, n):
        win = ext_ref[0, pl.ds(r0, n + SUBLANES), :]
        for s in range(1, SUBLANES):
            ext_ref[s, pl.ds(r0, n), :] = win[s:s + n]

    def shift(ci, carry):
        shift_rows(pl.multiple_of(ci * CONV_SHIFT_ROWS, CONV_SHIFT_ROWS), CONV_SHIFT_ROWS)
        return carry

    lax.fori_loop(0, bs // CONV_SHIFT_ROWS, shift, 0)
    shift_rows(bs, CONV_HALO - SUBLANES)

    groups = CONV_ROWS // SUBLANES

    def conv_rows(r):
        r0 = pl.multiple_of(r * CONV_ROWS, CONV_ROWS)
        accs = [jnp.zeros((SUBLANES, c), F32) for _ in range(groups)]
        for j in range(CONV_WIDTH):
            off = CONV_HALO - (CONV_WIDTH - 1) + j
            s, a = off % SUBLANES, off // SUBLANES
            w_j = w_ref[j * SUBLANES:(j + 1) * SUBLANES, :]
            for k in range(groups):
                start = pl.multiple_of(r0 + (a + k) * SUBLANES, SUBLANES)
                accs[k] = accs[k] + w_j * ext_ref[s, pl.ds(start, SUBLANES), :]
        return jnp.concatenate(accs, axis=0) + b_ref[...]

    def norm_rows(r, y):
        mu = jnp.mean(y, axis=-1, keepdims=True)
        var = jnp.mean(jnp.square(y - mu), axis=-1, keepdims=True)
        yn = (y - mu) * lax.rsqrt(var + 1e-5) * g_ref[...] + beta_ref[...]
        r0 = pl.multiple_of(r * CONV_ROWS, CONV_ROWS)
        o_ref[pl.ds(r0, CONV_ROWS), :] = (yn * _sigmoid(yn)).astype(o_ref.dtype)

    def rows(r, y_prev):
        y = conv_rows(r)
        norm_rows(r - 1, y_prev)
        return y

    n_chunks = bs // CONV_ROWS
    norm_rows(n_chunks - 1, lax.fori_loop(1, n_chunks, rows, conv_rows(0)))


def _conv_branch(glu, w, b, g, beta, *, seq, bs=512):
    t, c2 = glu.shape
    c = c2 // 2
    per = bs // CONV_HALO
    vec = lambda i: (0, 0)
    return pl.pallas_call(
        functools.partial(_conv_body, blocks_per_seq=seq // bs),
        grid=(t // bs,),
        in_specs=[
            pl.BlockSpec((bs, c2), lambda i: (i, 0)),
            pl.BlockSpec((CONV_HALO, c2), lambda i: (jnp.maximum(i * per - 1, 0), 0)),
            pl.BlockSpec((CONV_WIDTH * SUBLANES, c), vec),
            pl.BlockSpec((1, c), vec),
            pl.BlockSpec((1, c), vec),
            pl.BlockSpec((1, c), vec),
        ],
        out_specs=pl.BlockSpec((bs, c), lambda i: (i, 0)),
        out_shape=jax.ShapeDtypeStruct((t, c), BF16),
        scratch_shapes=[pltpu.VMEM((SUBLANES, CONV_HALO + bs, c), F32)],
        compiler_params=pltpu.CompilerParams(
            dimension_semantics=("parallel",), vmem_limit_bytes=VMEM_LIMIT),
        name="conv_branch",
    )(glu, glu, w, b, g, beta)


def _merge_body(x_ref, a_ref, u_ref, gate_ref, wa_ref, wc_ref, wo_ref, o_ref):
    d = o_ref.shape[1]
    ab = jnp.dot(a_ref[...], wa_ref[...], preferred_element_type=F32)
    cb = jnp.dot(u_ref[...], wc_ref[...], preferred_element_type=F32)
    gates = gate_ref[...]
    merged = _sigmoid(gates[:, :d]) * ab + _sigmoid(gates[:, d:]) * cb
    o_ref[...] = x_ref[...] + jnp.dot(merged.astype(BF16), wo_ref[...],
                                      preferred_element_type=F32)


def _merge(x1, attn_o, conv_u, gates, wa, wc, wo, *, bm=256):
    t, d = x1.shape
    row = lambda i: (i, 0)
    const = lambda i: (0, 0)
    resident = dict(pipeline_mode=pl.Buffered(1))
    return pl.pallas_call(
        _merge_body,
        grid=(t // bm,),
        in_specs=[
            pl.BlockSpec((bm, d), row),
            pl.BlockSpec((bm, attn_o.shape[1]), row),
            pl.BlockSpec((bm, conv_u.shape[1]), row),
            pl.BlockSpec((bm, 2 * d), row),
            pl.BlockSpec(wa.shape, const, **resident),
            pl.BlockSpec(wc.shape, const, **resident),
            pl.BlockSpec(wo.shape, const, **resident),
        ],
        out_specs=pl.BlockSpec((bm, d), row),
        out_shape=jax.ShapeDtypeStruct((t, d), F32),
        compiler_params=pltpu.CompilerParams(
            dimension_semantics=("parallel",), vmem_limit_bytes=VMEM_LIMIT),
        name="merge",
    )(x1, attn_o, conv_u, gates, wa, wc, wo)


def kernel(x, ffn1_norm_g, ffn1_w_gate, ffn1_w_up, ffn1_w_down, mix_norm_g, w_in, rel_bias_table, lambda_q1, lambda_k1, lambda_q2, lambda_k2, attn_head_norm_g, w_attn_branch, conv_dw_w, conv_dw_b, conv_ln_g, conv_ln_b, w_conv_branch, w_out, ffn2_norm_g, ffn2_w_gate, ffn2_w_up, ffn2_w_down, final_norm_g):
    batch, seq, d = x.shape
    depth = ffn1_w_gate.shape[0]
    attn_w = N_HEADS * HEAD_W
    conv_ch = conv_dw_w.shape[2]
    xt = x.reshape(batch * seq, d)
    row = lambda v: v.reshape(1, -1).astype(F32)

    for l in range(depth):
        lam_init = 0.8 - 0.6 * math.exp(-0.3 * l)
        last = l == depth - 1
        w_in_l = w_in[l]
        w_qk = w_in_l[:, :2 * attn_w].astype(BF16)
        w_vt = w_in_l[:, 2 * attn_w:3 * attn_w].T.astype(BF16)
        w_glu = w_in_l[:, 3 * attn_w:3 * attn_w + 2 * conv_ch].astype(BF16)
        w_gates = w_in_l[:, 3 * attn_w + 2 * conv_ch:].astype(BF16)

        x1, h2 = _ffn(xt, row(ffn1_norm_g[l]), ffn1_w_gate[l].astype(BF16),
                      ffn1_w_up[l].astype(BF16), ffn1_w_down[l].astype(BF16),
                      row(mix_norm_g[l]), final=False)

        qk = _mm(h2, w_qk, BF16, name="proj_qk")
        vt = _mm_nt(w_vt, h2, BF16, name="proj_vt")
        glu = _mm(h2, w_glu, F32, name="proj_glu")
        gates = _mm(h2, w_gates, F32, name="proj_gates")

        lam_vecs = jnp.stack([lambda_q1[l], lambda_k1[l], lambda_q2[l], lambda_k2[l]]).astype(F32)
        attn_o = _attention(qk, vt, rel_bias_table.astype(F32), lam_vecs,
                            attn_head_norm_g[l].reshape(HEAD_W, 1).astype(F32),
                            batch=batch, seq=seq, lam_init=lam_init)
        conv_w = jnp.repeat(conv_dw_w[l].astype(F32), SUBLANES, axis=0)
        conv_u = _conv_branch(glu, conv_w, row(conv_dw_b[l]),
                              row(conv_ln_g[l]), row(conv_ln_b[l]), seq=seq)
        x2 = _merge(x1, attn_o, conv_u, gates, w_attn_branch[l].astype(BF16),
                    w_conv_branch[l].astype(BF16), w_out[l].astype(BF16))

        res = _ffn(x2, row(ffn2_norm_g[l]), ffn2_w_gate[l].astype(BF16),
                   ffn2_w_up[l].astype(BF16), ffn2_w_down[l].astype(BF16),
                   row(final_norm_g), final=last)
        xt = res if last else res[0]

    return xt.reshape(batch, seq, d)
```

```python
import functools
import math

import jax
import jax.numpy as jnp
from jax import lax
from jax.experimental import pallas as pl
from jax.experimental.pallas import tpu as pltpu

F32 = jnp.float32
BF16 = jnp.bfloat16

N_HEADS = 8
HEAD_DIM = 64
HEAD_W = 2 * HEAD_DIM
CONV_WIDTH = 31
NUM_BUCKETS = 32
MAX_EXACT = NUM_BUCKETS // 2
MAX_DISTANCE = 128
NEG_INF = -1e30
LOG2E = math.log2(math.e)
LANES = 128
SUBLANES = 8
MIB = 1024 * 1024
VMEM_LIMIT = 56 * MIB

FFN_SPLIT = 2
ATT_BLK = 128
ATT_GROUP = 4
CONV_HALO = 32
CONV_ROWS = 32
CONV_SHIFT_ROWS = 64


def _rms(x, g, eps):
    return x * lax.rsqrt(jnp.mean(x * x, axis=-1, keepdims=True) + eps) * g


def _sigmoid(x):
    return 1.0 / (1.0 + jnp.exp(-x))


def _ffn_body(x_ref, g_ref, wg_ref, wu_ref, wd_ref, ng_ref, *rest, final):
    if final:
        o_ref, h_ref = rest
    else:
        o_ref, hn_ref, h_ref = rest
    f = pl.program_id(1)

    @pl.when(f == 0)
    def _():
        x = x_ref[...]
        h_ref[...] = _rms(x, g_ref[...], 1e-6).astype(BF16)
        o_ref[...] = x

    h = h_ref[...]
    tf = wg_ref.shape[1]
    halves = [slice(s * (tf // FFN_SPLIT), (s + 1) * (tf // FFN_SPLIT)) for s in range(FFN_SPLIT)]
    gu = [(jnp.dot(h, wg_ref[:, c], preferred_element_type=F32),
           jnp.dot(h, wu_ref[:, c], preferred_element_type=F32)) for c in halves]
    down = None
    for (gate, up), c in zip(gu, halves):
        act = ((gate * _sigmoid(gate)) * up * 0.5).astype(BF16)
        part = jnp.dot(act, wd_ref[c, :], preferred_element_type=F32)
        down = part if down is None else down + part
    o_ref[...] += down

    @pl.when(f == pl.num_programs(1) - 1)
    def _():
        y = _rms(o_ref[...], ng_ref[...], 1e-6)
        if final:
            o_ref[...] = y
        else:
            hn_ref[...] = y.astype(BF16)


def _ffn(x, g, wg, wu, wd, ng, *, final, bm=512, tf=512):
    t, d = x.shape
    dff = wg.shape[1]
    grid = (t // bm, dff // tf)
    row = lambda i, f: (i, 0)
    out_shape = [jax.ShapeDtypeStruct((t, d), F32)]
    out_specs = [pl.BlockSpec((bm, d), row)]
    if not final:
        out_shape.append(jax.ShapeDtypeStruct((t, d), BF16))
        out_specs.append(pl.BlockSpec((bm, d), row))
    res = pl.pallas_call(
        functools.partial(_ffn_body, final=final),
        grid=grid,
        in_specs=[
            pl.BlockSpec((bm, d), row),
            pl.BlockSpec((1, d), lambda i, f: (0, 0)),
            pl.BlockSpec((d, tf), lambda i, f: (0, f)),
            pl.BlockSpec((d, tf), lambda i, f: (0, f)),
            pl.BlockSpec((tf, d), lambda i, f: (f, 0)),
            pl.BlockSpec((1, d), lambda i, f: (0, 0)),
        ],
        out_specs=out_specs,
        out_shape=out_shape,
        scratch_shapes=[pltpu.VMEM((bm, d), BF16)],
        compiler_params=pltpu.CompilerParams(
            dimension_semantics=("parallel", "arbitrary"), vmem_limit_bytes=VMEM_LIMIT),
        name="ffn_final" if final else "ffn",
    )(x, g, wg, wu, wd, ng)
    return res[0] if final else res


def _mm_body(a_ref, b_ref, o_ref, *, tile_scales):
    acc = jnp.dot(a_ref[...], b_ref[...], preferred_element_type=F32)
    if tile_scales is not None:
        scale = jnp.float32(tile_scales[-1])
        for j, sc in enumerate(tile_scales[:-1]):
            scale = jnp.where(pl.program_id(1) == j, jnp.float32(sc), scale)
        acc = acc * scale
    o_ref[...] = acc.astype(o_ref.dtype)


def _mm(a, b, out_dtype, *, bm=1024, tn=1024, tile_scales=None, name):
    m, k = a.shape
    n = b.shape[1]
    assert tile_scales is None or len(tile_scales) == n // tn
    return pl.pallas_call(
        functools.partial(_mm_body, tile_scales=tile_scales),
        grid=(m // bm, n // tn),
        in_specs=[pl.BlockSpec((bm, k), lambda i, j: (i, 0)),
                  pl.BlockSpec((k, tn), lambda i, j: (0, j))],
        out_specs=pl.BlockSpec((bm, tn), lambda i, j: (i, j)),
        out_shape=jax.ShapeDtypeStruct((m, n), out_dtype),
        compiler_params=pltpu.CompilerParams(
            dimension_semantics=("parallel", "arbitrary"), vmem_limit_bytes=VMEM_LIMIT),
        name=name,
    )(a, b)


def _mm_nt_body(wt_ref, a_ref, o_ref):
    o_ref[...] = lax.dot_general(wt_ref[...], a_ref[...], (((1,), (1,)), ((), ())),
                                 preferred_element_type=F32).astype(o_ref.dtype)


def _mm_nt(wt, a, out_dtype, *, bm=1024, name):
    n, k = wt.shape
    m = a.shape[0]
    return pl.pallas_call(
        _mm_nt_body,
        grid=(m // bm,),
        in_specs=[pl.BlockSpec((n, k), lambda i: (0, 0)),
                  pl.BlockSpec((bm, k), lambda i: (i, 0))],
        out_specs=pl.BlockSpec((n, bm), lambda i: (0, i)),
        out_shape=jax.ShapeDtypeStruct((n, m), out_dtype),
        compiler_params=pltpu.CompilerParams(
            dimension_semantics=("parallel",), vmem_limit_bytes=VMEM_LIMIT),
        name=name,
    )(wt, a)


def _t5_bias(d, tbl_ref, h):
    n = jnp.maximum(d, 0)
    nf = jnp.maximum(n, 1).astype(F32)
    large = MAX_EXACT + (jnp.log(nf / MAX_EXACT) / math.log(MAX_DISTANCE / MAX_EXACT)
                         * (NUM_BUCKETS - MAX_EXACT)).astype(jnp.int32)
    large = jnp.minimum(large, NUM_BUCKETS - 1)
    bucket = jnp.where(n < MAX_EXACT, n, large)
    b = jnp.zeros(d.shape, F32)
    for j in range(NUM_BUCKETS):
        b = jnp.where(bucket == j, tbl_ref[j, h], b)
    b = (b - tbl_ref[NUM_BUCKETS - 1, h]) * LOG2E
    return jnp.where(d >= 0, b, NEG_INF)


def _attn_body(tbl_ref, lam_ref, g_ref, q_ref, k_ref, vt_ref, o_ref, bias_ref, *, lam_init, seq):
    h = pl.program_id(1)
    blk, grp = ATT_BLK, ATT_GROUP
    width = grp * blk
    nb = grp + 1
    kk = lax.broadcasted_iota(jnp.int32, (blk, blk), 0)
    jj = lax.broadcasted_iota(jnp.int32, (blk, blk), 1)
    diag = _t5_bias(jj - kk, tbl_ref, h)
    sub = _t5_bias(blk + jj - kk, tbl_ref, h)
    bias_ref[:(nb - 2) * blk, :] = jnp.zeros(((nb - 2) * blk, 2 * blk), F32)
    bias_ref[(nb - 2) * blk:(nb - 1) * blk, :] = jnp.concatenate([sub, sub], axis=1)
    bias_ref[(nb - 1) * blk:, :] = jnp.concatenate([diag, diag], axis=1)

    lv = lam_ref[...]
    lam = (jnp.exp(jnp.sum(lv[0:1] * lv[1:2], axis=-1, keepdims=True))
           - jnp.exp(jnp.sum(lv[2:3] * lv[3:4], axis=-1, keepdims=True)) + lam_init)
    g = g_ref[...]
    lane = lax.broadcasted_iota(jnp.int32, (blk, HEAD_W), 1)

    def load_qs(q0):
        out = []
        for r in range(grp):
            q = q_ref[pl.ds(q0 + r * blk, blk), :].astype(F32)
            out.append(jnp.concatenate([jnp.where(lane < HEAD_DIM, q, 0.0),
                                        jnp.where(lane >= HEAD_DIM, q, 0.0)],
                                       axis=0).astype(BF16))
        return out

    def scores(kb, qs, bias):
        st = lax.dot_general(kb, qs, (((1,), (1,)), ((), ())), preferred_element_type=F32)
        return st if bias is None else st + bias

    def update(state, st, vt):
        m, l, acc = state
        m_new = jnp.maximum(m, jnp.max(st, axis=0, keepdims=True))
        alpha = jnp.exp2(m - m_new)
        p = jnp.exp2(st - m_new)
        l = alpha * l + jnp.sum(p, axis=0, keepdims=True)
        acc = alpha * acc + jnp.dot(vt, p.astype(BF16), preferred_element_type=F32)
        return m_new, l, acc

    def finish(state, q0):
        _, l, acc = state
        o = acc * (1.0 / l)
        ot = o[:, :blk] - lam * o[:, blk:]
        y = ot * lax.rsqrt(jnp.mean(ot * ot, axis=0, keepdims=True) + 1e-5) * g
        y = y * (1.0 - lam_init)
        o_ref[pl.ds(q0, blk), :] = y.T.astype(o_ref.dtype)

    init = ((jnp.full((1, 2 * blk), NEG_INF, F32), jnp.zeros((1, 2 * blk), F32),
             jnp.zeros((HEAD_W, 2 * blk), F32)),) * grp

    def far_scores(qs, k0):
        kb = k_ref[pl.ds(k0, width), :]
        return tuple(scores(kb, qs[r], None) for r in range(grp))

    def prev_scores(qs, q0):
        kb = k_ref[pl.ds(q0 - width, width), :]
        return tuple(scores(kb, qs[r], bias_ref[:width, :] if r == 0 else None)
                     for r in range(grp))

    def own_scores(qs, q0):
        kb = k_ref[pl.ds(q0, width), :]
        return tuple(scores(kb[:(r + 1) * blk], qs[r], bias_ref[(grp - r) * blk:, :])
                     for r in range(grp))

    def updates(states, sts, k0):
        vt = vt_ref[:, pl.ds(k0, width)]
        return tuple(update(states[r], sts[r], vt[:, :sts[r].shape[0]]) for r in range(grp))

    def finish_all(states, q0):
        for r in range(grp):
            finish(states[r], q0 + r * blk)

    steps = []
    for gi in range(seq // width):
        q0 = gi * width
        kinds = ["far"] * max(gi - 1, 0) + (["prev"] if gi else []) + ["own"]
        for n, kind in enumerate(kinds):
            steps.append((gi, kind, n * width, n == len(kinds) - 1))

    def step_scores(step, qs):
        gi, kind, k0, _ = step
        if kind == "far":
            return far_scores(qs, k0)
        return prev_scores(qs, gi * width) if kind == "prev" else own_scores(qs, gi * width)

    qs = load_qs(0)
    cur = step_scores(steps[0], qs)
    states = init
    for n, step in enumerate(steps):
        gi, _, k0, last = step
        nxt = None
        if n + 1 < len(steps):
            if last:
                qs = load_qs(steps[n + 1][0] * width)
            nxt = step_scores(steps[n + 1], qs)
        states = updates(states, cur, k0)
        if last:
            finish_all(states, gi * width)
            states = init
        cur = nxt


def _attention(qk, vt, tbl, lam_vecs, g_col, *, batch, seq, lam_init):
    t = qk.shape[0]
    return pl.pallas_call(
        functools.partial(_attn_body, lam_init=lam_init, seq=seq),
        grid=(batch, N_HEADS),
        in_specs=[
            pl.BlockSpec(memory_space=pltpu.SMEM),
            pl.BlockSpec((4, HEAD_DIM), lambda b, h: (0, 0)),
            pl.BlockSpec((HEAD_W, 1), lambda b, h: (0, 0)),
            pl.BlockSpec((seq, HEAD_W), lambda b, h: (b, h)),
            pl.BlockSpec((seq, HEAD_W), lambda b, h: (b, N_HEADS + h)),
            pl.BlockSpec((HEAD_W, seq), lambda b, h: (h, b)),
        ],
        out_specs=pl.BlockSpec((seq, HEAD_W), lambda b, h: (b, h)),
        out_shape=jax.ShapeDtypeStruct((t, N_HEADS * HEAD_W), BF16),
        scratch_shapes=[pltpu.VMEM(((ATT_GROUP + 1) * ATT_BLK, 2 * ATT_BLK), F32)],
        compiler_params=pltpu.CompilerParams(
            dimension_semantics=("parallel", "parallel"), vmem_limit_bytes=VMEM_LIMIT),
        name="diff_attn",
    )(tbl, lam_vecs, g_col, qk, qk, vt)


def _conv_body(cur_ref, halo_ref, w_ref, b_ref, g_ref, beta_ref, o_ref, ext_ref, *, blocks_per_seq):
    i = pl.program_id(0)
    c = o_ref.shape[1]
    bs = o_ref.shape[0]
    cur = cur_ref[...]
    ext_ref[0, CONV_HALO:, :] = cur[:, :c] * _sigmoid(cur[:, c:])
    hal = halo_ref[...]
    uh = hal[:, :c] * _sigmoid(hal[:, c:])
    ext_ref[0, :CONV_HALO, :] = jnp.where(i % blocks_per_seq == 0, 0.0, uh)

    def shift_rows(r0, n):
        win = ext_ref[0, pl.ds(r0, n + SUBLANES), :]
        for s in range(1, SUBLANES):
            ext_ref[s, pl.ds(r0, n), :] = win[s:s + n]

    def shift(ci, carry):
        shift_rows(pl.multiple_of(ci * CONV_SHIFT_ROWS, CONV_SHIFT_ROWS), CONV_SHIFT_ROWS)
        return carry

    lax.fori_loop(0, bs // CONV_SHIFT_ROWS, shift, 0)
    shift_rows(bs, CONV_HALO - SUBLANES)

    groups = CONV_ROWS // SUBLANES

    def conv_rows(r):
        r0 = pl.multiple_of(r * CONV_ROWS, CONV_ROWS)
        accs = [jnp.zeros((SUBLANES, c), F32) for _ in range(groups)]
        for j in range(CONV_WIDTH):
            off = CONV_HALO - (CONV_WIDTH - 1) + j
            s, a = off % SUBLANES, off // SUBLANES
            w_j = w_ref[j * SUBLANES:(j + 1) * SUBLANES, :]
            for k in range(groups):
                start = pl.multiple_of(r0 + (a + k) * SUBLANES, SUBLANES)
                accs[k] = accs[k] + w_j * ext_ref[s, pl.ds(start, SUBLANES), :]
        return jnp.concatenate(accs, axis=0) + b_ref[...]

    def norm_rows(r, y):
        mu = jnp.mean(y, axis=-1, keepdims=True)
        var = jnp.mean(jnp.square(y - mu), axis=-1, keepdims=True)
        yn = (y - mu) * lax.rsqrt(var + 1e-5) * g_ref[...] + beta_ref[...]
        r0 = pl.multiple_of(r * CONV_ROWS, CONV_ROWS)
        o_ref[pl.ds(r0, CONV_ROWS), :] = (yn * _sigmoid(yn)).astype(o_ref.dtype)

    def rows(r, y_prev):
        y = conv_rows(r)
        norm_rows(r - 1, y_prev)
        return y

    n_chunks = bs // CONV_ROWS
    norm_rows(n_chunks - 1, lax.fori_loop(1, n_chunks, rows, conv_rows(0)))


def _conv_branch(glu, w, b, g, beta, *, seq, bs=512):
    t, c2 = glu.shape
    c = c2 // 2
    per = bs // CONV_HALO
    vec = lambda i: (0, 0)
    return pl.pallas_call(
        functools.partial(_conv_body, blocks_per_seq=seq // bs),
        grid=(t // bs,),
        in_specs=[
            pl.BlockSpec((bs, c2), lambda i: (i, 0)),
            pl.BlockSpec((CONV_HALO, c2), lambda i: (jnp.maximum(i * per - 1, 0), 0)),
            pl.BlockSpec((CONV_WIDTH * SUBLANES, c), vec),
            pl.BlockSpec((1, c), vec),
            pl.BlockSpec((1, c), vec),
            pl.BlockSpec((1, c), vec),
        ],
        out_specs=pl.BlockSpec((bs, c), lambda i: (i, 0)),
        out_shape=jax.ShapeDtypeStruct((t, c), BF16),
        scratch_shapes=[pltpu.VMEM((SUBLANES, CONV_HALO + bs, c), F32)],
        compiler_params=pltpu.CompilerParams(
            dimension_semantics=("parallel",), vmem_limit_bytes=VMEM_LIMIT),
        name="conv_branch",
    )(glu, glu, w, b, g, beta)


def _merge_body(x_ref, a_ref, u_ref, gate_ref, wa_ref, wc_ref, wo_ref, o_ref):
    d = o_ref.shape[1]
    ab = jnp.dot(a_ref[...], wa_ref[...], preferred_element_type=F32)
    cb = jnp.dot(u_ref[...], wc_ref[...], preferred_element_type=F32)
    gates = gate_ref[...]
    merged = _sigmoid(gates[:, :d]) * ab + _sigmoid(gates[:, d:]) * cb
    o_ref[...] = x_ref[...] + jnp.dot(merged.astype(BF16), wo_ref[...],
                                      preferred_element_type=F32)


def _merge(x1, attn_o, conv_u, gates, wa, wc, wo, *, bm=256):
    t, d = x1.shape
    row = lambda i: (i, 0)
    const = lambda i: (0, 0)
    resident = dict(pipeline_mode=pl.Buffered(1))
    return pl.pallas_call(
        _merge_body,
        grid=(t // bm,),
        in_specs=[
            pl.BlockSpec((bm, d), row),
            pl.BlockSpec((bm, attn_o.shape[1]), row),
            pl.BlockSpec((bm, conv_u.shape[1]), row),
            pl.BlockSpec((bm, 2 * d), row),
            pl.BlockSpec(wa.shape, const, **resident),
            pl.BlockSpec(wc.shape, const, **resident),
            pl.BlockSpec(wo.shape, const, **resident),
        ],
        out_specs=pl.BlockSpec((bm, d), row),
        out_shape=jax.ShapeDtypeStruct((t, d), F32),
        compiler_params=pltpu.CompilerParams(
            dimension_semantics=("parallel",), vmem_limit_bytes=VMEM_LIMIT),
        name="merge",
    )(x1, attn_o, conv_u, gates, wa, wc, wo)


def kernel(x, ffn1_norm_g, ffn1_w_gate, ffn1_w_up, ffn1_w_down, mix_norm_g, w_in, rel_bias_table, lambda_q1, lambda_k1, lambda_q2, lambda_k2, attn_head_norm_g, w_attn_branch, conv_dw_w, conv_dw_b, conv_ln_g, conv_ln_b, w_conv_branch, w_out, ffn2_norm_g, ffn2_w_gate, ffn2_w_up, ffn2_w_down, final_norm_g):
    batch, seq, d = x.shape
    depth = ffn1_w_gate.shape[0]
    attn_w = N_HEADS * HEAD_W
    conv_ch = conv_dw_w.shape[2]
    xt = x.reshape(batch * seq, d)
    row = lambda v: v.reshape(1, -1).astype(F32)

    for l in range(depth):
        lam_init = 0.8 - 0.6 * math.exp(-0.3 * l)
        last = l == depth - 1
        w_in_l = w_in[l]
        w_qk = w_in_l[:, :2 * attn_w].astype(BF16)
        w_vt = w_in_l[:, 2 * attn_w:3 * attn_w].T.astype(BF16)
        w_glu = w_in_l[:, 3 * attn_w:3 * attn_w + 2 * conv_ch].astype(BF16)
        w_gates = w_in_l[:, 3 * attn_w + 2 * conv_ch:].astype(BF16)

        x1, h2 = _ffn(xt, row(ffn1_norm_g[l]), ffn1_w_gate[l].astype(BF16),
                      ffn1_w_up[l].astype(BF16), ffn1_w_down[l].astype(BF16),
                      row(mix_norm_g[l]), final=False)

        qk = _mm(h2, w_qk, BF16, tile_scales=(HEAD_DIM ** -0.5, LOG2E), name="proj_qk")
        vt = _mm_nt(w_vt, h2, BF16, name="proj_vt")
        glu = _mm(h2, w_glu, F32, name="proj_glu")
        gates = _mm(h2, w_gates, F32, name="proj_gates")

        lam_vecs = jnp.stack([lambda_q1[l], lambda_k1[l], lambda_q2[l], lambda_k2[l]]).astype(F32)
        attn_o = _attention(qk, vt, rel_bias_table.astype(F32), lam_vecs,
                            attn_head_norm_g[l].reshape(HEAD_W, 1).astype(F32),
                            batch=batch, seq=seq, lam_init=lam_init)
        conv_w = jnp.repeat(conv_dw_w[l].astype(F32), SUBLANES, axis=0)
        conv_u = _conv_branch(glu, conv_w, row(conv_dw_b[l]),
                              row(conv_ln_g[l]), row(conv_ln_b[l]), seq=seq)
        x2 = _merge(x1, attn_o, conv_u, gates, w_attn_branch[l].astype(BF16),
                    w_conv_branch[l].astype(BF16), w_out[l].astype(BF16))

        res = _ffn(x2, row(ffn2_norm_g[l]), ffn2_w_gate[l].astype(BF16),
                   ffn2_w_up[l].astype(BF16), ffn2_w_down[l].astype(BF16),
                   row(final_norm_g), final=last)
        xt = res if last else res[0]

    return xt.reshape(batch, seq, d)
```

```python
import functools
import math

import jax
import jax.numpy as jnp
from jax import lax
from jax.experimental import pallas as pl
from jax.experimental.pallas import tpu as pltpu

F32 = jnp.float32
BF16 = jnp.bfloat16

N_HEADS = 8
HEAD_DIM = 64
HEAD_W = 2 * HEAD_DIM
CONV_WIDTH = 31
NUM_BUCKETS = 32
MAX_EXACT = NUM_BUCKETS // 2
MAX_DISTANCE = 128
NEG_INF = -1e30
LOG2E = math.log2(math.e)
LANES = 128
BF16_ROWS = 16
SUBLANES = 8
MXU_N = 256
MXU_K = 256
MIB = 1024 * 1024
VMEM_LIMIT = 56 * MIB

FFN_SPLIT = 2
ATT_BLK = 128
ATT_GROUP = 4
CONV_HALO = 32
CONV_SHIFT_ROWS = 64


def _rms(x, g, eps):
    return x * lax.rsqrt(jnp.mean(x * x, axis=-1, keepdims=True) + eps) * g


def _sigmoid(x):
    return 1.0 / (1.0 + jnp.exp(-x))


def _ffn_body(x_ref, g_ref, wg_ref, wu_ref, wd_ref, ng_ref, *rest, final):
    if final:
        o_ref, h_ref = rest
    else:
        o_ref, hn_ref, h_ref = rest
    f = pl.program_id(1)

    @pl.when(f == 0)
    def _():
        x = x_ref[...]
        h_ref[...] = _rms(x, g_ref[...], 1e-6).astype(BF16)
        o_ref[...] = x

    h = h_ref[...]
    tf = wg_ref.shape[1]
    halves = [slice(s * (tf // FFN_SPLIT), (s + 1) * (tf // FFN_SPLIT)) for s in range(FFN_SPLIT)]
    gu = [(jnp.dot(h, wg_ref[:, c], preferred_element_type=F32),
           jnp.dot(h, wu_ref[:, c], preferred_element_type=F32)) for c in halves]
    down = None
    for (gate, up), c in zip(gu, halves):
        act = ((gate * _sigmoid(gate)) * up * 0.5).astype(BF16)
        part = jnp.dot(act, wd_ref[c, :], preferred_element_type=F32)
        down = part if down is None else down + part
    o_ref[...] += down

    @pl.when(f == pl.num_programs(1) - 1)
    def _():
        y = _rms(o_ref[...], ng_ref[...], 1e-6)
        if final:
            o_ref[...] = y
        else:
            hn_ref[...] = y.astype(BF16)


def _ffn(x, g, wg, wu, wd, ng, *, final, bm=512, tf=512):
    t, d = x.shape
    dff = wg.shape[1]
    grid = (t // bm, dff // tf)
    row = lambda i, f: (i, 0)
    out_shape = [jax.ShapeDtypeStruct((t, d), F32)]
    out_specs = [pl.BlockSpec((bm, d), row)]
    if not final:
        out_shape.append(jax.ShapeDtypeStruct((t, d), BF16))
        out_specs.append(pl.BlockSpec((bm, d), row))
    return pl.pallas_call(
        functools.partial(_ffn_body, final=final),
        grid=grid,
        in_specs=[
            pl.BlockSpec((bm, d), row),
            pl.BlockSpec((1, d), lambda i, f: (0, 0)),
            pl.BlockSpec((d, tf), lambda i, f: (0, f)),
            pl.BlockSpec((d, tf), lambda i, f: (0, f)),
            pl.BlockSpec((tf, d), lambda i, f: (f, 0)),
            pl.BlockSpec((1, d), lambda i, f: (0, 0)),
        ],
        out_specs=out_specs,
        out_shape=out_shape,
        scratch_shapes=[pltpu.VMEM((bm, d), BF16)],
        compiler_params=pltpu.CompilerParams(
            dimension_semantics=("parallel", "arbitrary"), vmem_limit_bytes=VMEM_LIMIT),
        name="ffn_final" if final else "ffn",
    )(x, g, wg, wu, wd, ng)


def _mm_body(a_ref, b_ref, o_ref, *, tile_scales):
    acc = jnp.dot(a_ref[...], b_ref[...], preferred_element_type=F32)
    if tile_scales is not None:
        scale = jnp.float32(tile_scales[-1])
        for j, sc in enumerate(tile_scales[:-1]):
            scale = jnp.where(pl.program_id(1) == j, jnp.float32(sc), scale)
        acc = acc * scale
    o_ref[...] = acc.astype(o_ref.dtype)


def _mm(a, b, out_dtype, *, cols, bm=1024, tn=1024, tile_scales=None, name):
    m, k = a.shape
    n = cols[1] - cols[0]
    j0 = cols[0] // tn
    assert cols[0] % tn == 0 and n % tn == 0
    assert tile_scales is None or len(tile_scales) == n // tn
    return pl.pallas_call(
        functools.partial(_mm_body, tile_scales=tile_scales),
        grid=(m // bm, n // tn),
        in_specs=[pl.BlockSpec((bm, k), lambda i, j: (i, 0)),
                  pl.BlockSpec((k, tn), lambda i, j: (0, j0 + j))],
        out_specs=pl.BlockSpec((bm, tn), lambda i, j: (i, j)),
        out_shape=jax.ShapeDtypeStruct((m, n), out_dtype),
        compiler_params=pltpu.CompilerParams(
            dimension_semantics=("parallel", "arbitrary"), vmem_limit_bytes=VMEM_LIMIT),
        name=name,
    )(a, b)


def _mm_nt_body(wt_ref, a_ref, o_ref):
    o_ref[...] = lax.dot_general(wt_ref[...], a_ref[...], (((1,), (1,)), ((), ())),
                                 preferred_element_type=F32).astype(o_ref.dtype)


def _mm_nt(wt, a, out_dtype, *, bm=1024, name):
    n, k = wt.shape
    m = a.shape[0]
    return pl.pallas_call(
        _mm_nt_body,
        grid=(m // bm,),
        in_specs=[pl.BlockSpec((n, k), lambda i: (0, 0)),
                  pl.BlockSpec((bm, k), lambda i: (i, 0))],
        out_specs=pl.BlockSpec((n, bm), lambda i: (0, i)),
        out_shape=jax.ShapeDtypeStruct((n, m), out_dtype),
        compiler_params=pltpu.CompilerParams(
            dimension_semantics=("parallel",), vmem_limit_bytes=VMEM_LIMIT),
        name=name,
    )(wt, a)


def _t5_bias(d, tbl_ref, h):
    n = jnp.maximum(d, 0)
    nf = jnp.maximum(n, 1).astype(F32)
    large = MAX_EXACT + (jnp.log(nf / MAX_EXACT) / math.log(MAX_DISTANCE / MAX_EXACT)
                         * (NUM_BUCKETS - MAX_EXACT)).astype(jnp.int32)
    large = jnp.minimum(large, NUM_BUCKETS - 1)
    bucket = jnp.where(n < MAX_EXACT, n, large)
    b = jnp.zeros(d.shape, F32)
    for j in range(NUM_BUCKETS):
        b = jnp.where(bucket == j, tbl_ref[j, h], b)
    b = (b - tbl_ref[NUM_BUCKETS - 1, h]) * LOG2E
    return jnp.where(d >= 0, b, NEG_INF)


def _attn_body(tbl_ref, lam_ref, g_ref, q_ref, k_ref, vt_ref, *rest, lam_init, seq):
    n_cast = (len(rest) - 2) // 2
    cast_in, o_ref, cast_out, bias_ref = rest[:n_cast], rest[n_cast], rest[n_cast + 1:-1], rest[-1]
    for src, dst in zip(cast_in, cast_out):
        dst[...] = src[...].astype(BF16)

    h = pl.program_id(1)
    blk, grp = ATT_BLK, ATT_GROUP
    width = grp * blk
    nb = grp + 1
    kk = lax.broadcasted_iota(jnp.int32, (blk, blk), 0)
    jj = lax.broadcasted_iota(jnp.int32, (blk, blk), 1)
    diag = _t5_bias(jj - kk, tbl_ref, h)
    sub = _t5_bias(blk + jj - kk, tbl_ref, h)
    bias_ref[:(nb - 2) * blk, :] = jnp.zeros(((nb - 2) * blk, 2 * blk), F32)
    bias_ref[(nb - 2) * blk:(nb - 1) * blk, :] = jnp.concatenate([sub, sub], axis=1)
    bias_ref[(nb - 1) * blk:, :] = jnp.concatenate([diag, diag], axis=1)

    lv = lam_ref[...]
    lam = (jnp.exp(jnp.sum(lv[0:1] * lv[1:2], axis=-1, keepdims=True))
           - jnp.exp(jnp.sum(lv[2:3] * lv[3:4], axis=-1, keepdims=True)) + lam_init)
    g = g_ref[...]
    lane = lax.broadcasted_iota(jnp.int32, (blk, HEAD_W), 1)

    def load_qs(q0):
        out = []
        for r in range(grp):
            q = q_ref[pl.ds(q0 + r * blk, blk), :].astype(F32)
            out.append(jnp.concatenate([jnp.where(lane < HEAD_DIM, q, 0.0),
                                        jnp.where(lane >= HEAD_DIM, q, 0.0)],
                                       axis=0).astype(BF16))
        return out

    def scores(kb, qs, bias):
        st = lax.dot_general(kb, qs, (((1,), (1,)), ((), ())), preferred_element_type=F32)
        return st if bias is None else st + bias

    def update(state, st, vt):
        m, acc = state
        m_new = jnp.maximum(m, jnp.max(st, axis=0, keepdims=True))
        alpha = jnp.exp2(m - m_new)
        p = jnp.exp2(st - m_new)
        acc = alpha * acc + jnp.dot(vt, p.astype(BF16), preferred_element_type=F32)
        return m_new, acc

    def finish(state, q0):
        _, acc = state
        o = acc[:HEAD_W] * (1.0 / acc[HEAD_W:HEAD_W + 1])
        ot = o[:, :blk] - lam * o[:, blk:]
        y = ot * lax.rsqrt(jnp.mean(ot * ot, axis=0, keepdims=True) + 1e-5) * g
        y = y * (1.0 - lam_init)
        o_ref[pl.ds(q0, blk), :] = y.T.astype(o_ref.dtype)

    init = ((jnp.full((1, 2 * blk), NEG_INF, F32),
             jnp.zeros((HEAD_W + BF16_ROWS, 2 * blk), F32)),) * grp
    ones_rows = jnp.ones((BF16_ROWS, width), BF16)

    def far_scores(qs, k0):
        kb = k_ref[pl.ds(k0, width), :]
        return tuple(scores(kb, qs[r], None) for r in range(grp))

    def prev_scores(qs, q0):
        kb = k_ref[pl.ds(q0 - width, width), :]
        return tuple(scores(kb, qs[r], bias_ref[:width, :] if r == 0 else None)
                     for r in range(grp))

    def own_scores(qs, q0):
        kb = k_ref[pl.ds(q0, width), :]
        return tuple(scores(kb[:(r + 1) * blk], qs[r], bias_ref[(grp - r) * blk:, :])
                     for r in range(grp))

    def updates(states, sts, k0):
        vt = jnp.concatenate([vt_ref[:, pl.ds(k0, width)], ones_rows], axis=0)
        return tuple(update(states[r], sts[r], vt[:, :sts[r].shape[0]]) for r in range(grp))

    def finish_all(states, q0):
        for r in range(grp):
            finish(states[r], q0 + r * blk)

    steps = []
    for gi in range(seq // width):
        q0 = gi * width
        kinds = ["far"] * max(gi - 1, 0) + (["prev"] if gi else []) + ["own"]
        for n, kind in enumerate(kinds):
            steps.append((gi, kind, n * width, n == len(kinds) - 1))

    def step_scores(step, qs):
        gi, kind, k0, _ = step
        if kind == "far":
            return far_scores(qs, k0)
        return prev_scores(qs, gi * width) if kind == "prev" else own_scores(qs, gi * width)

    qs = load_qs(0)
    cur = step_scores(steps[0], qs)
    states = init
    for n, step in enumerate(steps):
        gi, _, k0, last = step
        nxt = None
        if n + 1 < len(steps):
            if last:
                qs = load_qs(steps[n + 1][0] * width)
            nxt = step_scores(steps[n + 1], qs)
        states = updates(states, cur, k0)
        if last:
            finish_all(states, gi * width)
            states = init
        cur = nxt


def _attention(qk, vt, tbl, lam_vecs, g_col, *, batch, seq, lam_init, casts=()):
    t = qk.shape[0]
    n_steps = batch * N_HEADS
    slab = lambda b, h: (b * N_HEADS + h, 0)
    for w in casts:
        assert w.shape[0] % (n_steps * BF16_ROWS) == 0, w.shape
    cast_specs = [pl.BlockSpec((w.shape[0] // n_steps, w.shape[1]), slab) for w in casts]
    return pl.pallas_call(
        functools.partial(_attn_body, lam_init=lam_init, seq=seq),
        grid=(batch, N_HEADS),
        in_specs=[
            pl.BlockSpec(memory_space=pltpu.SMEM),
            pl.BlockSpec((4, HEAD_DIM), lambda b, h: (0, 0)),
            pl.BlockSpec((HEAD_W, 1), lambda b, h: (0, 0)),
            pl.BlockSpec((seq, HEAD_W), lambda b, h: (b, h)),
            pl.BlockSpec((seq, HEAD_W), lambda b, h: (b, N_HEADS + h)),
            pl.BlockSpec((HEAD_W, seq), lambda b, h: (h, b)),
        ] + cast_specs,
        out_specs=[pl.BlockSpec((seq, HEAD_W), lambda b, h: (b, h))] + cast_specs,
        out_shape=[jax.ShapeDtypeStruct((t, N_HEADS * HEAD_W), BF16)]
        + [jax.ShapeDtypeStruct(w.shape, BF16) for w in casts],
        scratch_shapes=[pltpu.VMEM(((ATT_GROUP + 1) * ATT_BLK, 2 * ATT_BLK), F32)],
        compiler_params=pltpu.CompilerParams(
            dimension_semantics=("parallel", "parallel"), vmem_limit_bytes=VMEM_LIMIT),
        name="diff_attn",
    )(tbl, lam_vecs, g_col, qk, qk, vt, *casts)


def _conv_prepare(first, cur_ref, halo_ref, ext_ref, bs, c):
    cur = cur_ref[...]
    ext_ref[0, CONV_HALO:, :] = cur[:, :c] * _sigmoid(cur[:, c:])
    hal = halo_ref[...]
    uh = hal[:, :c] * _sigmoid(hal[:, c:])
    ext_ref[0, :CONV_HALO, :] = jnp.where(first, 0.0, uh)

    def shift_rows(r0, n):
        win = ext_ref[0, r0:r0 + n + SUBLANES, :]
        for s in range(1, SUBLANES):
            ext_ref[s, r0:r0 + n, :] = win[s:s + n]

    for r0 in range(0, bs, CONV_SHIFT_ROWS):
        shift_rows(r0, CONV_SHIFT_ROWS)
    shift_rows(bs, CONV_HALO - SUBLANES)


def _conv_group(r0, w_ref, b_ref, g_ref, beta_ref, o_ref, ext_ref):
    acc = jnp.zeros((SUBLANES, o_ref.shape[1]), F32)
    for j in range(CONV_WIDTH):
        off = CONV_HALO - (CONV_WIDTH - 1) + j
        s, a = off % SUBLANES, off // SUBLANES
        start = r0 + a * SUBLANES
        acc = acc + w_ref[j * SUBLANES:(j + 1) * SUBLANES, :] * ext_ref[s, start:start + SUBLANES, :]
    y = acc + b_ref[...]
    mu = jnp.mean(y, axis=-1, keepdims=True)
    var = jnp.mean(jnp.square(y - mu), axis=-1, keepdims=True)
    yn = (y - mu) * lax.rsqrt(var + 1e-5) * g_ref[...] + beta_ref[...]
    out = yn * _sigmoid(yn)
    o_ref[r0:r0 + SUBLANES, :] = out.astype(o_ref.dtype)
    return out


def _zero_after(x):
    bits = pltpu.bitcast(jnp.concatenate([x[:SUBLANES, :LANES]] * 2, axis=0), jnp.uint32)
    bits = lax.shift_right_logical(lax.shift_right_logical(bits, jnp.uint32(16)), jnp.uint32(16))
    return pltpu.bitcast(bits, F32).astype(BF16)


def _gates_conv_body(h_ref, wg_ref, cur_ref, halo_ref, w_ref, b_ref, g_ref, beta_ref,
                     gate_o_ref, conv_o_ref, ext_ref, *, blocks_per_seq):
    bs, c = conv_o_ref.shape
    tn = wg_ref.shape[1]
    blk = pl.program_id(0) * pl.num_programs(1) + pl.program_id(1)
    _conv_prepare(blk % blocks_per_seq == 0, cur_ref, halo_ref, ext_ref, bs, c)
    n_chunks = tn // MXU_N
    k_pieces = h_ref.shape[1] // MXU_K
    per_piece = (bs // SUBLANES) // (n_chunks * k_pieces)
    anchor = None
    r0 = 0
    for ci in range(n_chunks):
        cols = slice(ci * MXU_N, (ci + 1) * MXU_N)
        acc = None
        for ki in range(k_pieces):
            rows = slice(ki * MXU_K, (ki + 1) * MXU_K)
            w = wg_ref[rows, cols]
            if anchor is not None:
                w = jnp.concatenate([w[:16, :LANES] + anchor, w[:16, LANES:]], axis=1)
                w = jnp.concatenate([w, wg_ref[ki * MXU_K + 16:(ki + 1) * MXU_K, cols]], axis=0)
            part = jnp.dot(h_ref[:, rows], w, preferred_element_type=F32)
            acc = part if acc is None else acc + part
            for _ in range(per_piece):
                out = _conv_group(r0, w_ref, b_ref, g_ref, beta_ref, conv_o_ref, ext_ref)
                r0 += SUBLANES
            anchor = _zero_after(out)
        gate_o_ref[:, cols] = acc
    assert r0 == bs


def _gates_conv(h2, w_in, glu, w, b, g, beta, *, cols, seq, bm=1024, tn=1024):
    t, k = h2.shape
    n = cols[1] - cols[0]
    j0 = cols[0] // tn
    assert cols[0] % tn == 0 and n % tn == 0
    c2 = glu.shape[1]
    c = c2 // 2
    steps = n // tn
    bs = bm // steps
    per = bs // CONV_HALO
    vec = lambda i, j: (0, 0)
    return pl.pallas_call(
        functools.partial(_gates_conv_body, blocks_per_seq=seq // bs),
        grid=(t // bm, steps),
        in_specs=[
            pl.BlockSpec((bm, k), lambda i, j: (i, 0)),
            pl.BlockSpec((k, tn), lambda i, j: (0, j0 + j)),
            pl.BlockSpec((bs, c2), lambda i, j: (i * steps + j, 0)),
            pl.BlockSpec((CONV_HALO, c2),
                         lambda i, j: (jnp.maximum((i * steps + j) * per - 1, 0), 0)),
            pl.BlockSpec((CONV_WIDTH * SUBLANES, c), vec),
            pl.BlockSpec((1, c), vec),
            pl.BlockSpec((1, c), vec),
            pl.BlockSpec((1, c), vec),
        ],
        out_specs=[pl.BlockSpec((bm, tn), lambda i, j: (i, j)),
                   pl.BlockSpec((bs, c), lambda i, j: (i * steps + j, 0))],
        out_shape=[jax.ShapeDtypeStruct((t, n), F32), jax.ShapeDtypeStruct((t, c), BF16)],
        scratch_shapes=[pltpu.VMEM((SUBLANES, CONV_HALO + bs, c), F32)],
        compiler_params=pltpu.CompilerParams(
            dimension_semantics=("parallel", "arbitrary"), vmem_limit_bytes=VMEM_LIMIT),
        name="gates_conv",
    )(h2, w_in, glu, glu, w, b, g, beta)


def _merge_body(x_ref, a_ref, u_ref, gate_ref, wa_ref, wc_ref, wo_ref, o_ref):
    d = o_ref.shape[1]
    ab = jnp.dot(a_ref[...], wa_ref[...], preferred_element_type=F32)
    cb = jnp.dot(u_ref[...], wc_ref[...], preferred_element_type=F32)
    gates = gate_ref[...]
    merged = _sigmoid(gates[:, :d]) * ab + _sigmoid(gates[:, d:]) * cb
    o_ref[...] = x_ref[...] + jnp.dot(merged.astype(BF16), wo_ref[...],
                                      preferred_element_type=F32)


def _merge(x1, attn_o, conv_u, gates, wa, wc, wo, *, bm=256):
    t, d = x1.shape
    row = lambda i: (i, 0)
    const = lambda i: (0, 0)
    resident = dict(pipeline_mode=pl.Buffered(1))
    return pl.pallas_call(
        _merge_body,
        grid=(t // bm,),
        in_specs=[
            pl.BlockSpec((bm, d), row),
            pl.BlockSpec((bm, attn_o.shape[1]), row),
            pl.BlockSpec((bm, conv_u.shape[1]), row),
            pl.BlockSpec((bm, 2 * d), row),
            pl.BlockSpec(wa.shape, const, **resident),
            pl.BlockSpec(wc.shape, const, **resident),
            pl.BlockSpec(wo.shape, const, **resident),
        ],
        out_specs=pl.BlockSpec((bm, d), row),
        out_shape=jax.ShapeDtypeStruct((t, d), F32),
        compiler_params=pltpu.CompilerParams(
            dimension_semantics=("parallel",), vmem_limit_bytes=VMEM_LIMIT),
        name="merge",
    )(x1, attn_o, conv_u, gates, wa, wc, wo)


def kernel(x, ffn1_norm_g, ffn1_w_gate, ffn1_w_up, ffn1_w_down, mix_norm_g, w_in, rel_bias_table, lambda_q1, lambda_k1, lambda_q2, lambda_k2, attn_head_norm_g, w_attn_branch, conv_dw_w, conv_dw_b, conv_ln_g, conv_ln_b, w_conv_branch, w_out, ffn2_norm_g, ffn2_w_gate, ffn2_w_up, ffn2_w_down, final_norm_g):
    batch, seq, d = x.shape
    depth = ffn1_w_gate.shape[0]
    attn_w = N_HEADS * HEAD_W
    conv_ch = conv_dw_w.shape[2]
    xt = x.reshape(batch * seq, d)
    row = lambda v: v.reshape(1, -1).astype(F32)

    for l in range(depth):
        lam_init = 0.8 - 0.6 * math.exp(-0.3 * l)
        last = l == depth - 1
        qk_cols = (0, 2 * attn_w)
        glu_cols = (3 * attn_w, 3 * attn_w + 2 * conv_ch)
        gate_cols = (glu_cols[1], w_in.shape[2])
        w_in_bf = w_in[l].astype(BF16)
        w_vt = w_in[l][:, 2 * attn_w:3 * attn_w].T.astype(BF16)

        x1, h2 = _ffn(xt, row(ffn1_norm_g[l]), ffn1_w_gate[l].astype(BF16),
                      ffn1_w_up[l].astype(BF16), ffn1_w_down[l].astype(BF16),
                      row(mix_norm_g[l]), final=False)

        qk = _mm(h2, w_in_bf, BF16, cols=qk_cols, tile_scales=(HEAD_DIM ** -0.5, LOG2E),
                 name="proj_qk")
        vt = _mm_nt(w_vt, h2, BF16, name="proj_vt")
        glu = _mm(h2, w_in_bf, F32, cols=glu_cols, name="proj_glu")
        conv_w = jnp.repeat(conv_dw_w[l].astype(F32), SUBLANES, axis=0)
        gates, conv_u = _gates_conv(h2, w_in_bf, glu, conv_w, row(conv_dw_b[l]),
                                    row(conv_ln_g[l]), row(conv_ln_b[l]), cols=gate_cols,
                                    seq=seq)

        later = (w_attn_branch[l], w_conv_branch[l], w_out[l],
                 ffn2_w_gate[l], ffn2_w_up[l], ffn2_w_down[l])
        lam_vecs = jnp.stack([lambda_q1[l], lambda_k1[l], lambda_q2[l], lambda_k2[l]]).astype(F32)
        attn_o, w_a, w_c, w_o, w2_gate, w2_up, w2_down = _attention(
            qk, vt, rel_bias_table.astype(F32), lam_vecs,
            attn_head_norm_g[l].reshape(HEAD_W, 1).astype(F32),
            batch=batch, seq=seq, lam_init=lam_init, casts=tuple(w.astype(F32) for w in later))
        x2 = _merge(x1, attn_o, conv_u, gates, w_a, w_c, w_o)

        xt = _ffn(x2, row(ffn2_norm_g[l]), w2_gate, w2_up, w2_down, row(final_norm_g),
                  final=last)[0]

    return xt.reshape(batch, seq, d)
```

```python
import functools
import math

import jax
import jax.numpy as jnp
from jax import lax
from jax.experimental import pallas as pl
from jax.experimental.pallas import tpu as pltpu

F32 = jnp.float32
BF16 = jnp.bfloat16

N_HEADS = 8
HEAD_DIM = 64
HEAD_W = 2 * HEAD_DIM
CONV_WIDTH = 31
NUM_BUCKETS = 32
MAX_EXACT = NUM_BUCKETS // 2
MAX_DISTANCE = 128
NEG_INF = -1e30
LOG2E = math.log2(math.e)
LANES = 128
BF16_ROWS = 16
SUBLANES = 8
MXU_N = 256
MXU_K = 256
MIB = 1024 * 1024
VMEM_LIMIT = 56 * MIB
FFN_VMEM_LIMIT = 60 * MIB

FFN_SPLIT = 2
ATT_BLK = 128
ATT_GROUP = 4
CONV_HALO = 32
CONV_SHIFT_ROWS = 64


def _rms(x, g, eps):
    return x * lax.rsqrt(jnp.mean(x * x, axis=-1, keepdims=True) + eps) * g


def _sigmoid(x):
    return 1.0 / (1.0 + jnp.exp(-x))


def _ffn_body(x_ref, g_ref, wg_ref, wu_ref, wd_ref, ng_ref, *rest, final):
    if final:
        o_ref, h_ref = rest
    else:
        o_ref, hn_ref = rest
        h_ref = hn_ref
    f = pl.program_id(1)

    @pl.when(f == 0)
    def _():
        x = x_ref[...]
        h_ref[...] = _rms(x, g_ref[...], 1e-6).astype(BF16)
        o_ref[...] = x

    h = h_ref[...]
    tf = wg_ref.shape[1]
    halves = [slice(s * (tf // FFN_SPLIT), (s + 1) * (tf // FFN_SPLIT)) for s in range(FFN_SPLIT)]
    gu = [(jnp.dot(h, wg_ref[:, c], preferred_element_type=F32),
           jnp.dot(h, wu_ref[:, c], preferred_element_type=F32)) for c in halves]
    down = None
    for (gate, up), c in zip(gu, halves):
        act = ((gate * _sigmoid(gate)) * up * 0.5).astype(BF16)
        part = jnp.dot(act, wd_ref[c, :], preferred_element_type=F32)
        down = part if down is None else down + part
    o_ref[...] += down

    @pl.when(f == pl.num_programs(1) - 1)
    def _():
        y = _rms(o_ref[...], ng_ref[...], 1e-6)
        if final:
            o_ref[...] = y
        else:
            hn_ref[...] = y.astype(BF16)


def _ffn(x, g, wg, wu, wd, ng, *, final, bm=1024, tf=512):
    t, d = x.shape
    dff = wg.shape[1]
    grid = (t // bm, dff // tf)
    row = lambda i, f: (i, 0)
    out_shape = [jax.ShapeDtypeStruct((t, d), F32)]
    out_specs = [pl.BlockSpec((bm, d), row)]
    if not final:
        out_shape.append(jax.ShapeDtypeStruct((t, d), BF16))
        out_specs.append(pl.BlockSpec((bm, d), row))
    return pl.pallas_call(
        functools.partial(_ffn_body, final=final),
        grid=grid,
        in_specs=[
            pl.BlockSpec((bm, d), row),
            pl.BlockSpec((1, d), lambda i, f: (0, 0)),
            pl.BlockSpec((d, tf), lambda i, f: (0, f)),
            pl.BlockSpec((d, tf), lambda i, f: (0, f)),
            pl.BlockSpec((tf, d), lambda i, f: (f, 0)),
            pl.BlockSpec((1, d), lambda i, f: (0, 0)),
        ],
        out_specs=out_specs,
        out_shape=out_shape,
        scratch_shapes=[pltpu.VMEM((bm, d), BF16)] if final else [],
        compiler_params=pltpu.CompilerParams(
            dimension_semantics=("parallel", "arbitrary"), vmem_limit_bytes=FFN_VMEM_LIMIT),
        name="ffn_final" if final else "ffn",
    )(x, g, wg, wu, wd, ng)


def _mm_body(a_ref, b_ref, o_ref, *, tile_scales):
    acc = jnp.dot(a_ref[...], b_ref[...], preferred_element_type=F32)
    if tile_scales is not None:
        scale = jnp.float32(tile_scales[-1])
        for j, sc in enumerate(tile_scales[:-1]):
            scale = jnp.where(pl.program_id(1) == j, jnp.float32(sc), scale)
        acc = acc * scale
    o_ref[...] = acc.astype(o_ref.dtype)


def _mm(a, b, out_dtype, *, cols, bm=1024, tn=1024, tile_scales=None, name):
    m, k = a.shape
    n = cols[1] - cols[0]
    j0 = cols[0] // tn
    assert cols[0] % tn == 0 and n % tn == 0
    assert tile_scales is None or len(tile_scales) == n // tn
    return pl.pallas_call(
        functools.partial(_mm_body, tile_scales=tile_scales),
        grid=(m // bm, n // tn),
        in_specs=[pl.BlockSpec((bm, k), lambda i, j: (i, 0)),
                  pl.BlockSpec((k, tn), lambda i, j: (0, j0 + j))],
        out_specs=pl.BlockSpec((bm, tn), lambda i, j: (i, j)),
        out_shape=jax.ShapeDtypeStruct((m, n), out_dtype),
        compiler_params=pltpu.CompilerParams(
            dimension_semantics=("parallel", "arbitrary"), vmem_limit_bytes=VMEM_LIMIT),
        name=name,
    )(a, b)


def _mm_nt_body(wt_ref, a_ref, o_ref):
    o_ref[...] = lax.dot_general(wt_ref[...], a_ref[...], (((1,), (1,)), ((), ())),
                                 preferred_element_type=F32).astype(o_ref.dtype)


def _mm_nt(wt, a, out_dtype, *, bm=1024, name):
    n, k = wt.shape
    m = a.shape[0]
    return pl.pallas_call(
        _mm_nt_body,
        grid=(m // bm,),
        in_specs=[pl.BlockSpec((n, k), lambda i: (0, 0)),
                  pl.BlockSpec((bm, k), lambda i: (i, 0))],
        out_specs=pl.BlockSpec((n, bm), lambda i: (0, i)),
        out_shape=jax.ShapeDtypeStruct((n, m), out_dtype),
        compiler_params=pltpu.CompilerParams(
            dimension_semantics=("parallel",), vmem_limit_bytes=VMEM_LIMIT),
        name=name,
    )(wt, a)


def _t5_bias(d, tbl_ref, h):
    n = jnp.maximum(d, 0)
    nf = jnp.maximum(n, 1).astype(F32)
    large = MAX_EXACT + (jnp.log(nf / MAX_EXACT) / math.log(MAX_DISTANCE / MAX_EXACT)
                         * (NUM_BUCKETS - MAX_EXACT)).astype(jnp.int32)
    large = jnp.minimum(large, NUM_BUCKETS - 1)
    bucket = jnp.where(n < MAX_EXACT, n, large)
    b = jnp.zeros(d.shape, F32)
    for j in range(NUM_BUCKETS):
        b = jnp.where(bucket == j, tbl_ref[j, h], b)
    b = (b - tbl_ref[NUM_BUCKETS - 1, h]) * LOG2E
    return jnp.where(d >= 0, b, NEG_INF)


def _attn_body(tbl_ref, lam_ref, g_ref, q_ref, k_ref, vt_ref, *rest, lam_init, seq):
    n_cast = (len(rest) - 2) // 2
    cast_in, o_ref, cast_out, bias_ref = rest[:n_cast], rest[n_cast], rest[n_cast + 1:-1], rest[-1]
    for src, dst in zip(cast_in, cast_out):
        dst[...] = src[...].astype(BF16)

    h = pl.program_id(1)
    blk, grp = ATT_BLK, ATT_GROUP
    width = grp * blk
    nb = grp + 1
    kk = lax.broadcasted_iota(jnp.int32, (blk, blk), 0)
    jj = lax.broadcasted_iota(jnp.int32, (blk, blk), 1)
    diag = _t5_bias(jj - kk, tbl_ref, h)
    sub = _t5_bias(blk + jj - kk, tbl_ref, h)
    bias_ref[:(nb - 2) * blk, :] = jnp.zeros(((nb - 2) * blk, 2 * blk), F32)
    bias_ref[(nb - 2) * blk:(nb - 1) * blk, :] = jnp.concatenate([sub, sub], axis=1)
    bias_ref[(nb - 1) * blk:, :] = jnp.concatenate([diag, diag], axis=1)

    lv = lam_ref[...]
    lam = (jnp.exp(jnp.sum(lv[0:1] * lv[1:2], axis=-1, keepdims=True))
           - jnp.exp(jnp.sum(lv[2:3] * lv[3:4], axis=-1, keepdims=True)) + lam_init)
    g = g_ref[...]
    lane = lax.broadcasted_iota(jnp.int32, (blk, HEAD_W), 1)

    def load_qs(q0):
        out = []
        for r in range(grp):
            q = q_ref[pl.ds(q0 + r * blk, blk), :].astype(F32)
            out.append(jnp.concatenate([jnp.where(lane < HEAD_DIM, q, 0.0),
                                        jnp.where(lane >= HEAD_DIM, q, 0.0)],
                                       axis=0).astype(BF16))
        return out

    def scores(kb, qs, bias):
        st = lax.dot_general(kb, qs, (((1,), (1,)), ((), ())), preferred_element_type=F32)
        return st if bias is None else st + bias

    def update(state, st, vt):
        m, acc = state
        m_new = jnp.maximum(m, jnp.max(st, axis=0, keepdims=True))
        alpha = jnp.exp2(m - m_new)
        p = jnp.exp2(st - m_new)
        acc = alpha * acc + jnp.dot(vt, p.astype(BF16), preferred_element_type=F32)
        return m_new, acc

    def finish(state, q0):
        _, acc = state
        o = acc[:HEAD_W] * (1.0 / acc[HEAD_W:HEAD_W + 1])
        ot = o[:, :blk] - lam * o[:, blk:]
        y = ot * lax.rsqrt(jnp.mean(ot * ot, axis=0, keepdims=True) + 1e-5) * g
        y = y * (1.0 - lam_init)
        o_ref[pl.ds(q0, blk), :] = y.T.astype(o_ref.dtype)

    init = ((jnp.full((1, 2 * blk), NEG_INF, F32),
             jnp.zeros((HEAD_W + BF16_ROWS, 2 * blk), F32)),) * grp
    ones_rows = jnp.ones((BF16_ROWS, width), BF16)

    def far_scores(qs, k0):
        kb = k_ref[pl.ds(k0, width), :]
        return tuple(scores(kb, qs[r], None) for r in range(grp))

    def prev_scores(qs, q0):
        kb = k_ref[pl.ds(q0 - width, width), :]
        return tuple(scores(kb, qs[r], bias_ref[:width, :] if r == 0 else None)
                     for r in range(grp))

    def own_scores(qs, q0):
        kb = k_ref[pl.ds(q0, width), :]
        return tuple(scores(kb[:(r + 1) * blk], qs[r], bias_ref[(grp - r) * blk:, :])
                     for r in range(grp))

    def updates(states, sts, k0):
        vt = jnp.concatenate([vt_ref[:, pl.ds(k0, width)], ones_rows], axis=0)
        return tuple(update(states[r], sts[r], vt[:, :sts[r].shape[0]]) for r in range(grp))

    def finish_all(states, q0):
        for r in range(grp):
            finish(states[r], q0 + r * blk)

    steps = []
    for gi in range(seq // width):
        q0 = gi * width
        kinds = ["far"] * max(gi - 1, 0) + (["prev"] if gi else []) + ["own"]
        for n, kind in enumerate(kinds):
            steps.append((gi, kind, n * width, n == len(kinds) - 1))

    def step_scores(step, qs):
        gi, kind, k0, _ = step
        if kind == "far":
            return far_scores(qs, k0)
        return prev_scores(qs, gi * width) if kind == "prev" else own_scores(qs, gi * width)

    qs = load_qs(0)
    cur = step_scores(steps[0], qs)
    states = init
    for n, step in enumerate(steps):
        gi, _, k0, last = step
        nxt = None
        if n + 1 < len(steps):
            if last:
                qs = load_qs(steps[n + 1][0] * width)
            nxt = step_scores(steps[n + 1], qs)
        states = updates(states, cur, k0)
        if last:
            finish_all(states, gi * width)
            states = init
        cur = nxt


def _attention(qk, vt, tbl, lam_vecs, g_col, *, batch, seq, lam_init, casts=()):
    t = qk.shape[0]
    n_steps = batch * N_HEADS
    slab = lambda b, h: (b * N_HEADS + h, 0)
    for w in casts:
        assert w.shape[0] % (n_steps * BF16_ROWS) == 0, w.shape
    cast_specs = [pl.BlockSpec((w.shape[0] // n_steps, w.shape[1]), slab) for w in casts]
    return pl.pallas_call(
        functools.partial(_attn_body, lam_init=lam_init, seq=seq),
        grid=(batch, N_HEADS),
        in_specs=[
            pl.BlockSpec(memory_space=pltpu.SMEM),
            pl.BlockSpec((4, HEAD_DIM), lambda b, h: (0, 0)),
            pl.BlockSpec((HEAD_W, 1), lambda b, h: (0, 0)),
            pl.BlockSpec((seq, HEAD_W), lambda b, h: (b, h)),
            pl.BlockSpec((seq, HEAD_W), lambda b, h: (b, N_HEADS + h)),
            pl.BlockSpec((HEAD_W, seq), lambda b, h: (h, b)),
        ] + cast_specs,
        out_specs=[pl.BlockSpec((seq, HEAD_W), lambda b, h: (b, h))] + cast_specs,
        out_shape=[jax.ShapeDtypeStruct((t, N_HEADS * HEAD_W), BF16)]
        + [jax.ShapeDtypeStruct(w.shape, BF16) for w in casts],
        scratch_shapes=[pltpu.VMEM(((ATT_GROUP + 1) * ATT_BLK, 2 * ATT_BLK), F32)],
        compiler_params=pltpu.CompilerParams(
            dimension_semantics=("parallel", "parallel"), vmem_limit_bytes=VMEM_LIMIT),
        name="diff_attn",
    )(tbl, lam_vecs, g_col, qk, qk, vt, *casts)


def _conv_prepare(first, cur_ref, halo_ref, ext_ref, bs, c):
    cur = cur_ref[...]
    ext_ref[0, CONV_HALO:, :] = cur[:, :c] * _sigmoid(cur[:, c:])
    hal = halo_ref[...]
    uh = hal[:, :c] * _sigmoid(hal[:, c:])
    ext_ref[0, :CONV_HALO, :] = jnp.where(first, 0.0, uh)

    def shift_rows(r0, n):
        win = ext_ref[0, r0:r0 + n + SUBLANES, :]
        for s in range(1, SUBLANES):
            ext_ref[s, r0:r0 + n, :] = win[s:s + n]

    for r0 in range(0, bs, CONV_SHIFT_ROWS):
        shift_rows(r0, CONV_SHIFT_ROWS)
    shift_rows(bs, CONV_HALO - SUBLANES)


def _conv_group(r0, w_ref, b_ref, g_ref, beta_ref, o_ref, ext_ref):
    acc = jnp.zeros((SUBLANES, o_ref.shape[1]), F32)
    for j in range(CONV_WIDTH):
        off = CONV_HALO - (CONV_WIDTH - 1) + j
        s, a = off % SUBLANES, off // SUBLANES
        start = r0 + a * SUBLANES
        acc = acc + w_ref[j * SUBLANES:(j + 1) * SUBLANES, :] * ext_ref[s, start:start + SUBLANES, :]
    y = acc + b_ref[...]
    mu = jnp.mean(y, axis=-1, keepdims=True)
    var = jnp.mean(jnp.square(y - mu), axis=-1, keepdims=True)
    yn = (y - mu) * lax.rsqrt(var + 1e-5) * g_ref[...] + beta_ref[...]
    out = yn * _sigmoid(yn)
    o_ref[r0:r0 + SUBLANES, :] = out.astype(o_ref.dtype)
    return out


def _zero_after(x):
    bits = pltpu.bitcast(jnp.concatenate([x[:SUBLANES, :LANES]] * 2, axis=0), jnp.uint32)
    bits = lax.shift_right_logical(lax.shift_right_logical(bits, jnp.uint32(16)), jnp.uint32(16))
    return pltpu.bitcast(bits, F32).astype(BF16)


def _gates_conv_body(h_ref, wg_ref, cur_ref, halo_ref, w_ref, b_ref, g_ref, beta_ref,
                     gate_o_ref, conv_o_ref, ext_ref, *, blocks_per_seq):
    bs, c = conv_o_ref.shape
    tn = wg_ref.shape[1]
    blk = pl.program_id(0) * pl.num_programs(1) + pl.program_id(1)
    _conv_prepare(blk % blocks_per_seq == 0, cur_ref, halo_ref, ext_ref, bs, c)
    n_chunks = tn // MXU_N
    k_pieces = h_ref.shape[1] // MXU_K
    per_piece = (bs // SUBLANES) // (n_chunks * k_pieces)
    anchor = None
    r0 = 0
    for ci in range(n_chunks):
        cols = slice(ci * MXU_N, (ci + 1) * MXU_N)
        acc = None
        for ki in range(k_pieces):
            rows = slice(ki * MXU_K, (ki + 1) * MXU_K)
            w = wg_ref[rows, cols]
            if anchor is not None:
                top = jnp.concatenate([w[:BF16_ROWS, :LANES] + anchor, w[:BF16_ROWS, LANES:]],
                                      axis=1)
                w = jnp.concatenate(
                    [top, wg_ref[ki * MXU_K + BF16_ROWS:(ki + 1) * MXU_K, cols]], axis=0)
            part = jnp.dot(h_ref[:, rows], w, preferred_element_type=F32)
            acc = part if acc is None else acc + part
            for _ in range(per_piece):
                out = _conv_group(r0, w_ref, b_ref, g_ref, beta_ref, conv_o_ref, ext_ref)
                r0 += SUBLANES
            anchor = _zero_after(out)
        gate_o_ref[:, cols] = acc
    assert r0 == bs


def _gates_conv(h2, w_in, glu, w, b, g, beta, *, cols, seq, bm=1024, tn=1024):
    t, k = h2.shape
    n = cols[1] - cols[0]
    j0 = cols[0] // tn
    assert cols[0] % tn == 0 and n % tn == 0
    c2 = glu.shape[1]
    c = c2 // 2
    steps = n // tn
    bs = bm // steps
    per = bs // CONV_HALO
    vec = lambda i, j: (0, 0)
    return pl.pallas_call(
        functools.partial(_gates_conv_body, blocks_per_seq=seq // bs),
        grid=(t // bm, steps),
        in_specs=[
            pl.BlockSpec((bm, k), lambda i, j: (i, 0)),
            pl.BlockSpec((k, tn), lambda i, j: (0, j0 + j)),
            pl.BlockSpec((bs, c2), lambda i, j: (i * steps + j, 0)),
            pl.BlockSpec((CONV_HALO, c2),
                         lambda i, j: (jnp.maximum((i * steps + j) * per - 1, 0), 0)),
            pl.BlockSpec((CONV_WIDTH * SUBLANES, c), vec),
            pl.BlockSpec((1, c), vec),
            pl.BlockSpec((1, c), vec),
            pl.BlockSpec((1, c), vec),
        ],
        out_specs=[pl.BlockSpec((bm, tn), lambda i, j: (i, j)),
                   pl.BlockSpec((bs, c), lambda i, j: (i * steps + j, 0))],
        out_shape=[jax.ShapeDtypeStruct((t, n), F32), jax.ShapeDtypeStruct((t, c), BF16)],
        scratch_shapes=[pltpu.VMEM((SUBLANES, CONV_HALO + bs, c), F32)],
        compiler_params=pltpu.CompilerParams(
            dimension_semantics=("parallel", "arbitrary"), vmem_limit_bytes=VMEM_LIMIT),
        name="gates_conv",
    )(h2, w_in, glu, glu, w, b, g, beta)


def _merge_body(x_ref, a_ref, u_ref, gate_ref, wa_ref, wc_ref, wo_ref, o_ref):
    d = o_ref.shape[1]
    ab = jnp.dot(a_ref[...], wa_ref[...], preferred_element_type=F32)
    cb = jnp.dot(u_ref[...], wc_ref[...], preferred_element_type=F32)
    gates = gate_ref[...]
    merged = _sigmoid(gates[:, :d]) * ab + _sigmoid(gates[:, d:]) * cb
    o_ref[...] = x_ref[...] + jnp.dot(merged.astype(BF16), wo_ref[...],
                                      preferred_element_type=F32)


def _merge(x1, attn_o, conv_u, gates, wa, wc, wo, *, bm=256):
    t, d = x1.shape
    row = lambda i: (i, 0)
    const = lambda i: (0, 0)
    resident = dict(pipeline_mode=pl.Buffered(1))
    return pl.pallas_call(
        _merge_body,
        grid=(t // bm,),
        in_specs=[
            pl.BlockSpec((bm, d), row),
            pl.BlockSpec((bm, attn_o.shape[1]), row),
            pl.BlockSpec((bm, conv_u.shape[1]), row),
            pl.BlockSpec((bm, 2 * d), row),
            pl.BlockSpec(wa.shape, const, **resident),
            pl.BlockSpec(wc.shape, const, **resident),
            pl.BlockSpec(wo.shape, const, **resident),
        ],
        out_specs=pl.BlockSpec((bm, d), row),
        out_shape=jax.ShapeDtypeStruct((t, d), F32),
        compiler_params=pltpu.CompilerParams(
            dimension_semantics=("parallel",), vmem_limit_bytes=VMEM_LIMIT),
        name="merge",
    )(x1, attn_o, conv_u, gates, wa, wc, wo)


def kernel(x, ffn1_norm_g, ffn1_w_gate, ffn1_w_up, ffn1_w_down, mix_norm_g, w_in, rel_bias_table, lambda_q1, lambda_k1, lambda_q2, lambda_k2, attn_head_norm_g, w_attn_branch, conv_dw_w, conv_dw_b, conv_ln_g, conv_ln_b, w_conv_branch, w_out, ffn2_norm_g, ffn2_w_gate, ffn2_w_up, ffn2_w_down, final_norm_g):
    batch, seq, d = x.shape
    depth = ffn1_w_gate.shape[0]
    attn_w = N_HEADS * HEAD_W
    conv_ch = conv_dw_w.shape[2]
    xt = x.reshape(batch * seq, d)
    row = lambda v: v.reshape(1, -1).astype(F32)

    for l in range(depth):
        lam_init = 0.8 - 0.6 * math.exp(-0.3 * l)
        last = l == depth - 1
        qk_cols = (0, 2 * attn_w)
        glu_cols = (3 * attn_w, 3 * attn_w + 2 * conv_ch)
        gate_cols = (glu_cols[1], w_in.shape[2])
        w_in_bf = w_in[l].astype(BF16)
        w_vt = w_in[l][:, 2 * attn_w:3 * attn_w].T.astype(BF16)

        x1, h2 = _ffn(xt, row(ffn1_norm_g[l]), ffn1_w_gate[l].astype(BF16),
                      ffn1_w_up[l].astype(BF16), ffn1_w_down[l].astype(BF16),
                      row(mix_norm_g[l]), final=False)

        qk = _mm(h2, w_in_bf, BF16, cols=qk_cols, tile_scales=(HEAD_DIM ** -0.5, LOG2E),
                 name="proj_qk")
        vt = _mm_nt(w_vt, h2, BF16, name="proj_vt")
        glu = _mm(h2, w_in_bf, F32, cols=glu_cols, name="proj_glu")
        conv_w = jnp.repeat(conv_dw_w[l].astype(F32), SUBLANES, axis=0)
        gates, conv_u = _gates_conv(h2, w_in_bf, glu, conv_w, row(conv_dw_b[l]),
                                    row(conv_ln_g[l]), row(conv_ln_b[l]), cols=gate_cols,
                                    seq=seq)

        later = (w_attn_branch[l], w_conv_branch[l], w_out[l],
                 ffn2_w_gate[l], ffn2_w_up[l], ffn2_w_down[l])
        lam_vecs = jnp.stack([lambda_q1[l], lambda_k1[l], lambda_q2[l], lambda_k2[l]]).astype(F32)
        attn_o, w_a, w_c, w_o, w2_gate, w2_up, w2_down = _attention(
            qk, vt, rel_bias_table.astype(F32), lam_vecs,
            attn_head_norm_g[l].reshape(HEAD_W, 1).astype(F32),
            batch=batch, seq=seq, lam_init=lam_init, casts=tuple(w.astype(F32) for w in later))
        x2 = _merge(x1, attn_o, conv_u, gates, w_a, w_c, w_o)

        xt = _ffn(x2, row(ffn2_norm_g[l]), w2_gate, w2_up, w2_down, row(final_norm_g),
                  final=last)[0]

    return xt.reshape(batch, seq, d)
```

```python
import functools
import math

import jax
import jax.numpy as jnp
from jax import lax
from jax.experimental import pallas as pl
from jax.experimental.pallas import tpu as pltpu

F32 = jnp.float32
BF16 = jnp.bfloat16

N_HEADS = 8
HEAD_DIM = 64
HEAD_W = 2 * HEAD_DIM
CONV_WIDTH = 31
NUM_BUCKETS = 32
MAX_EXACT = NUM_BUCKETS // 2
MAX_DISTANCE = 128
NEG_INF = -1e30
LOG2E = math.log2(math.e)
LANES = 128
BF16_ROWS = 16
SUBLANES = 8
MXU_N = 256
MXU_K = 256
MIB = 1024 * 1024
VMEM_LIMIT = 56 * MIB
BIG_VMEM_LIMIT = 60 * MIB

FFN_SPLIT = 2
ATT_BLK = 128
ATT_GROUP = 4
CONV_HALO = 32
CONV_SHIFT_ROWS = 64


def _rms(x, g, eps):
    return x * lax.rsqrt(jnp.mean(x * x, axis=-1, keepdims=True) + eps) * g


def _sigmoid(x):
    return 1.0 / (1.0 + jnp.exp(-x))


def _ffn_body(x_ref, g_ref, wg_ref, wu_ref, wd_ref, ng_ref, *rest, final):
    if final:
        o_ref, h_ref = rest
    else:
        o_ref, hn_ref = rest
        h_ref = hn_ref
    f = pl.program_id(1)

    @pl.when(f == 0)
    def _():
        x = x_ref[...]
        h_ref[...] = _rms(x, g_ref[...], 1e-6).astype(BF16)
        o_ref[...] = x

    h = h_ref[...]
    tf = wg_ref.shape[1]
    halves = [slice(s * (tf // FFN_SPLIT), (s + 1) * (tf // FFN_SPLIT)) for s in range(FFN_SPLIT)]
    gu = [(jnp.dot(h, wg_ref[:, c], preferred_element_type=F32),
           jnp.dot(h, wu_ref[:, c], preferred_element_type=F32)) for c in halves]
    down = None
    for (gate, up), c in zip(gu, halves):
        act = ((gate * _sigmoid(gate)) * up * 0.5).astype(BF16)
        part = jnp.dot(act, wd_ref[c, :], preferred_element_type=F32)
        down = part if down is None else down + part
    o_ref[...] += down

    @pl.when(f == pl.num_programs(1) - 1)
    def _():
        y = _rms(o_ref[...], ng_ref[...], 1e-6)
        if final:
            o_ref[...] = y
        else:
            hn_ref[...] = y.astype(BF16)


def _ffn(x, g, wg, wu, wd, ng, *, final, bm=1024, tf=512):
    t, d = x.shape
    dff = wg.shape[1]
    grid = (t // bm, dff // tf)
    row = lambda i, f: (i, 0)
    out_shape = [jax.ShapeDtypeStruct((t, d), F32)]
    out_specs = [pl.BlockSpec((bm, d), row)]
    if not final:
        out_shape.append(jax.ShapeDtypeStruct((t, d), BF16))
        out_specs.append(pl.BlockSpec((bm, d), row))
    return pl.pallas_call(
        functools.partial(_ffn_body, final=final),
        grid=grid,
        in_specs=[
            pl.BlockSpec((bm, d), row),
            pl.BlockSpec((1, d), lambda i, f: (0, 0)),
            pl.BlockSpec((d, tf), lambda i, f: (0, f)),
            pl.BlockSpec((d, tf), lambda i, f: (0, f)),
            pl.BlockSpec((tf, d), lambda i, f: (f, 0)),
            pl.BlockSpec((1, d), lambda i, f: (0, 0)),
        ],
        out_specs=out_specs,
        out_shape=out_shape,
        scratch_shapes=[pltpu.VMEM((bm, d), BF16)] if final else [],
        compiler_params=pltpu.CompilerParams(
            dimension_semantics=("parallel", "arbitrary"), vmem_limit_bytes=BIG_VMEM_LIMIT),
        name="ffn_final" if final else "ffn",
    )(x, g, wg, wu, wd, ng)


def _mm_body(a_ref, b_ref, o_ref, *, tile_scales):
    acc = jnp.dot(a_ref[...], b_ref[...], preferred_element_type=F32)
    if tile_scales is not None:
        scale = jnp.float32(tile_scales[-1])
        for j, sc in enumerate(tile_scales[:-1]):
            scale = jnp.where(pl.program_id(1) == j, jnp.float32(sc), scale)
        acc = acc * scale
    o_ref[...] = acc.astype(o_ref.dtype)


def _mm(a, b, out_dtype, *, cols, bm=2048, tn=1024, tile_scales=None, name):
    m, k = a.shape
    n = cols[1] - cols[0]
    j0 = cols[0] // tn
    assert cols[0] % tn == 0 and n % tn == 0
    assert tile_scales is None or len(tile_scales) == n // tn
    return pl.pallas_call(
        functools.partial(_mm_body, tile_scales=tile_scales),
        grid=(m // bm, n // tn),
        in_specs=[pl.BlockSpec((bm, k), lambda i, j: (i, 0)),
                  pl.BlockSpec((k, tn), lambda i, j: (0, j0 + j))],
        out_specs=pl.BlockSpec((bm, tn), lambda i, j: (i, j)),
        out_shape=jax.ShapeDtypeStruct((m, n), out_dtype),
        compiler_params=pltpu.CompilerParams(
            dimension_semantics=("parallel", "arbitrary"), vmem_limit_bytes=VMEM_LIMIT),
        name=name,
    )(a, b)


def _mm_nt_body(wt_ref, a_ref, o_ref):
    o_ref[...] = lax.dot_general(wt_ref[...], a_ref[...], (((1,), (1,)), ((), ())),
                                 preferred_element_type=F32).astype(o_ref.dtype)


def _mm_nt(wt, a, out_dtype, *, bm=2048, name):
    n, k = wt.shape
    m = a.shape[0]
    return pl.pallas_call(
        _mm_nt_body,
        grid=(m // bm,),
        in_specs=[pl.BlockSpec((n, k), lambda i: (0, 0)),
                  pl.BlockSpec((bm, k), lambda i: (i, 0))],
        out_specs=pl.BlockSpec((n, bm), lambda i: (0, i)),
        out_shape=jax.ShapeDtypeStruct((n, m), out_dtype),
        compiler_params=pltpu.CompilerParams(
            dimension_semantics=("parallel",), vmem_limit_bytes=VMEM_LIMIT),
        name=name,
    )(wt, a)


def _t5_bias(d, tbl_ref, h):
    n = jnp.maximum(d, 0)
    nf = jnp.maximum(n, 1).astype(F32)
    large = MAX_EXACT + (jnp.log(nf / MAX_EXACT) / math.log(MAX_DISTANCE / MAX_EXACT)
                         * (NUM_BUCKETS - MAX_EXACT)).astype(jnp.int32)
    large = jnp.minimum(large, NUM_BUCKETS - 1)
    bucket = jnp.where(n < MAX_EXACT, n, large)
    b = jnp.zeros(d.shape, F32)
    for j in range(NUM_BUCKETS):
        b = jnp.where(bucket == j, tbl_ref[j, h], b)
    b = (b - tbl_ref[NUM_BUCKETS - 1, h]) * LOG2E
    return jnp.where(d >= 0, b, NEG_INF)


def _attn_body(tbl_ref, lam_ref, g_ref, q_ref, k_ref, vt_ref, *rest, lam_init, seq):
    n_cast = (len(rest) - 2) // 2
    cast_in, o_ref, cast_out, bias_ref = rest[:n_cast], rest[n_cast], rest[n_cast + 1:-1], rest[-1]
    for src, dst in zip(cast_in, cast_out):
        dst[...] = src[...].astype(BF16)

    h = pl.program_id(1)
    blk, grp = ATT_BLK, ATT_GROUP
    width = grp * blk
    nb = grp + 1
    kk = lax.broadcasted_iota(jnp.int32, (blk, blk), 0)
    jj = lax.broadcasted_iota(jnp.int32, (blk, blk), 1)
    diag = _t5_bias(jj - kk, tbl_ref, h)
    sub = _t5_bias(blk + jj - kk, tbl_ref, h)
    bias_ref[:(nb - 2) * blk, :] = jnp.zeros(((nb - 2) * blk, 2 * blk), F32)
    bias_ref[(nb - 2) * blk:(nb - 1) * blk, :] = jnp.concatenate([sub, sub], axis=1)
    bias_ref[(nb - 1) * blk:, :] = jnp.concatenate([diag, diag], axis=1)

    lv = lam_ref[...]
    lam = (jnp.exp(jnp.sum(lv[0:1] * lv[1:2], axis=-1, keepdims=True))
           - jnp.exp(jnp.sum(lv[2:3] * lv[3:4], axis=-1, keepdims=True)) + lam_init)
    g = g_ref[...]
    lane = lax.broadcasted_iota(jnp.int32, (blk, HEAD_W), 1)

    def load_qs(q0):
        out = []
        for r in range(grp):
            q = q_ref[pl.ds(q0 + r * blk, blk), :].astype(F32)
            out.append(jnp.concatenate([jnp.where(lane < HEAD_DIM, q, 0.0),
                                        jnp.where(lane >= HEAD_DIM, q, 0.0)],
                                       axis=0).astype(BF16))
        return out

    def scores(kb, qs, bias):
        st = lax.dot_general(kb, qs, (((1,), (1,)), ((), ())), preferred_element_type=F32)
        return st if bias is None else st + bias

    def update(state, st, vt):
        m, acc = state
        m_new = jnp.maximum(m, jnp.max(st, axis=0, keepdims=True))
        alpha = jnp.exp2(m - m_new)
        p = jnp.exp2(st - m_new)
        acc = alpha * acc + jnp.dot(vt, p.astype(BF16), preferred_element_type=F32)
        return m_new, acc

    def finish(state, q0):
        _, acc = state
        o = acc[:HEAD_W] * (1.0 / acc[HEAD_W:HEAD_W + 1])
        ot = o[:, :blk] - lam * o[:, blk:]
        y = ot * lax.rsqrt(jnp.mean(ot * ot, axis=0, keepdims=True) + 1e-5) * g
        y = y * (1.0 - lam_init)
        o_ref[pl.ds(q0, blk), :] = y.T.astype(o_ref.dtype)

    init = ((jnp.full((1, 2 * blk), NEG_INF, F32),
             jnp.zeros((HEAD_W + BF16_ROWS, 2 * blk), F32)),) * grp
    ones_rows = jnp.ones((BF16_ROWS, width), BF16)

    def far_scores(qs, k0):
        kb = k_ref[pl.ds(k0, width), :]
        return tuple(scores(kb, qs[r], None) for r in range(grp))

    def prev_scores(qs, q0):
        kb = k_ref[pl.ds(q0 - width, width), :]
        return tuple(scores(kb, qs[r], bias_ref[:width, :] if r == 0 else None)
                     for r in range(grp))

    def own_scores(qs, q0):
        kb = k_ref[pl.ds(q0, width), :]
        return tuple(scores(kb[:(r + 1) * blk], qs[r], bias_ref[(grp - r) * blk:, :])
                     for r in range(grp))

    def updates(states, sts, k0):
        vt = jnp.concatenate([vt_ref[:, pl.ds(k0, width)], ones_rows], axis=0)
        return tuple(update(states[r], sts[r], vt[:, :sts[r].shape[0]]) for r in range(grp))

    def finish_all(states, q0):
        for r in range(grp):
            finish(states[r], q0 + r * blk)

    steps = []
    for gi in range(seq // width):
        q0 = gi * width
        kinds = ["far"] * max(gi - 1, 0) + (["prev"] if gi else []) + ["own"]
        for n, kind in enumerate(kinds):
            steps.append((gi, kind, n * width, n == len(kinds) - 1))

    def step_scores(step, qs):
        gi, kind, k0, _ = step
        if kind == "far":
            return far_scores(qs, k0)
        return prev_scores(qs, gi * width) if kind == "prev" else own_scores(qs, gi * width)

    qs = load_qs(0)
    cur = step_scores(steps[0], qs)
    states = init
    for n, step in enumerate(steps):
        gi, _, k0, last = step
        nxt = None
        if n + 1 < len(steps):
            if last:
                qs = load_qs(steps[n + 1][0] * width)
            nxt = step_scores(steps[n + 1], qs)
        states = updates(states, cur, k0)
        if last:
            finish_all(states, gi * width)
            states = init
        cur = nxt


def _attention(qk, vt, tbl, lam_vecs, g_col, *, batch, seq, lam_init, casts=()):
    t = qk.shape[0]
    n_steps = batch * N_HEADS
    slab = lambda b, h: (b * N_HEADS + h, 0)
    for w in casts:
        assert w.shape[0] % (n_steps * BF16_ROWS) == 0, w.shape
    cast_specs = [pl.BlockSpec((w.shape[0] // n_steps, w.shape[1]), slab) for w in casts]
    return pl.pallas_call(
        functools.partial(_attn_body, lam_init=lam_init, seq=seq),
        grid=(batch, N_HEADS),
        in_specs=[
            pl.BlockSpec(memory_space=pltpu.SMEM),
            pl.BlockSpec((4, HEAD_DIM), lambda b, h: (0, 0)),
            pl.BlockSpec((HEAD_W, 1), lambda b, h: (0, 0)),
            pl.BlockSpec((seq, HEAD_W), lambda b, h: (b, h)),
            pl.BlockSpec((seq, HEAD_W), lambda b, h: (b, N_HEADS + h)),
            pl.BlockSpec((HEAD_W, seq), lambda b, h: (h, b)),
        ] + cast_specs,
        out_specs=[pl.BlockSpec((seq, HEAD_W), lambda b, h: (b, h))] + cast_specs,
        out_shape=[jax.ShapeDtypeStruct((t, N_HEADS * HEAD_W), BF16)]
        + [jax.ShapeDtypeStruct(w.shape, BF16) for w in casts],
        scratch_shapes=[pltpu.VMEM(((ATT_GROUP + 1) * ATT_BLK, 2 * ATT_BLK), F32)],
        compiler_params=pltpu.CompilerParams(
            dimension_semantics=("parallel", "parallel"), vmem_limit_bytes=VMEM_LIMIT),
        name="diff_attn",
    )(tbl, lam_vecs, g_col, qk, qk, vt, *casts)


def _conv_prepare(first, cur_ref, halo_ref, ext_ref, bs, c):
    cur = cur_ref[...]
    ext_ref[0, CONV_HALO:, :] = cur[:, :c] * _sigmoid(cur[:, c:])
    hal = halo_ref[...]
    uh = hal[:, :c] * _sigmoid(hal[:, c:])
    ext_ref[0, :CONV_HALO, :] = jnp.where(first, 0.0, uh)

    def shift_rows(r0, n):
        win = ext_ref[0, r0:r0 + n + SUBLANES, :]
        for s in range(1, SUBLANES):
            ext_ref[s, r0:r0 + n, :] = win[s:s + n]

    for r0 in range(0, bs, CONV_SHIFT_ROWS):
        shift_rows(r0, CONV_SHIFT_ROWS)
    shift_rows(bs, CONV_HALO - SUBLANES)


def _conv_group(r0, w_ref, b_ref, g_ref, beta_ref, o_ref, ext_ref):
    acc = jnp.zeros((SUBLANES, o_ref.shape[1]), F32)
    for j in range(CONV_WIDTH):
        off = CONV_HALO - (CONV_WIDTH - 1) + j
        s, a = off % SUBLANES, off // SUBLANES
        start = r0 + a * SUBLANES
        acc = acc + w_ref[j * SUBLANES:(j + 1) * SUBLANES, :] * ext_ref[s, start:start + SUBLANES, :]
    y = acc + b_ref[...]
    mu = jnp.mean(y, axis=-1, keepdims=True)
    var = jnp.mean(jnp.square(y - mu), axis=-1, keepdims=True)
    yn = (y - mu) * lax.rsqrt(var + 1e-5) * g_ref[...] + beta_ref[...]
    out = yn * _sigmoid(yn)
    o_ref[r0:r0 + SUBLANES, :] = out.astype(o_ref.dtype)
    return out


def _zero_after(x):
    bits = pltpu.bitcast(jnp.concatenate([x[:SUBLANES, :LANES]] * 2, axis=0), jnp.uint32)
    bits = lax.shift_right_logical(lax.shift_right_logical(bits, jnp.uint32(16)), jnp.uint32(16))
    return pltpu.bitcast(bits, F32).astype(BF16)


def _gates_conv_body(h_ref, wg_ref, cur_ref, halo_ref, w_ref, b_ref, g_ref, beta_ref,
                     gate_o_ref, conv_o_ref, ext_ref, *, blocks_per_seq):
    bs, c = conv_o_ref.shape
    tn = wg_ref.shape[1]
    blk = pl.program_id(0) * pl.num_programs(1) + pl.program_id(1)
    _conv_prepare(blk % blocks_per_seq == 0, cur_ref, halo_ref, ext_ref, bs, c)
    n_chunks = tn // MXU_N
    k_pieces = h_ref.shape[1] // MXU_K
    per_piece = (bs // SUBLANES) // (n_chunks * k_pieces)
    anchor = None
    r0 = 0
    for ci in range(n_chunks):
        cols = slice(ci * MXU_N, (ci + 1) * MXU_N)
        acc = None
        for ki in range(k_pieces):
            rows = slice(ki * MXU_K, (ki + 1) * MXU_K)
            w = wg_ref[rows, cols]
            if anchor is not None:
                top = jnp.concatenate([w[:BF16_ROWS, :LANES] + anchor, w[:BF16_ROWS, LANES:]],
                                      axis=1)
                w = jnp.concatenate(
                    [top, wg_ref[ki * MXU_K + BF16_ROWS:(ki + 1) * MXU_K, cols]], axis=0)
            part = jnp.dot(h_ref[:, rows], w, preferred_element_type=F32)
            acc = part if acc is None else acc + part
            for _ in range(per_piece):
                out = _conv_group(r0, w_ref, b_ref, g_ref, beta_ref, conv_o_ref, ext_ref)
                r0 += SUBLANES
            anchor = _zero_after(out)
        gate_o_ref[:, cols] = acc
    assert r0 == bs


def _gates_conv(h2, w_in, glu, w, b, g, beta, *, cols, seq, bm=1024, tn=1024):
    t, k = h2.shape
    n = cols[1] - cols[0]
    j0 = cols[0] // tn
    assert cols[0] % tn == 0 and n % tn == 0
    c2 = glu.shape[1]
    c = c2 // 2
    steps = n // tn
    bs = bm // steps
    per = bs // CONV_HALO
    vec = lambda i, j: (0, 0)
    return pl.pallas_call(
        functools.partial(_gates_conv_body, blocks_per_seq=seq // bs),
        grid=(t // bm, steps),
        in_specs=[
            pl.BlockSpec((bm, k), lambda i, j: (i, 0)),
            pl.BlockSpec((k, tn), lambda i, j: (0, j0 + j)),
            pl.BlockSpec((bs, c2), lambda i, j: (i * steps + j, 0)),
            pl.BlockSpec((CONV_HALO, c2),
                         lambda i, j: (jnp.maximum((i * steps + j) * per - 1, 0), 0)),
            pl.BlockSpec((CONV_WIDTH * SUBLANES, c), vec),
            pl.BlockSpec((1, c), vec),
            pl.BlockSpec((1, c), vec),
            pl.BlockSpec((1, c), vec),
        ],
        out_specs=[pl.BlockSpec((bm, tn), lambda i, j: (i, j)),
                   pl.BlockSpec((bs, c), lambda i, j: (i * steps + j, 0))],
        out_shape=[jax.ShapeDtypeStruct((t, n), F32), jax.ShapeDtypeStruct((t, c), BF16)],
        scratch_shapes=[pltpu.VMEM((SUBLANES, CONV_HALO + bs, c), F32)],
        compiler_params=pltpu.CompilerParams(
            dimension_semantics=("parallel", "arbitrary"), vmem_limit_bytes=VMEM_LIMIT),
        name="gates_conv",
    )(h2, w_in, glu, glu, w, b, g, beta)


def _merge_body(x_ref, a_ref, u_ref, gate_ref, wa_ref, wc_ref, wo_ref, o_ref):
    d = o_ref.shape[1]
    ab = jnp.dot(a_ref[...], wa_ref[...], preferred_element_type=F32)
    cb = jnp.dot(u_ref[...], wc_ref[...], preferred_element_type=F32)
    gates = gate_ref[...]
    merged = _sigmoid(gates[:, :d]) * ab + _sigmoid(gates[:, d:]) * cb
    o_ref[...] = x_ref[...] + jnp.dot(merged.astype(BF16), wo_ref[...],
                                      preferred_element_type=F32)


def _merge(x1, attn_o, conv_u, gates, wa, wc, wo, *, bm=512):
    t, d = x1.shape
    row = lambda i: (i, 0)
    const = lambda i: (0, 0)
    resident = dict(pipeline_mode=pl.Buffered(1))
    return pl.pallas_call(
        _merge_body,
        grid=(t // bm,),
        in_specs=[
            pl.BlockSpec((bm, d), row),
            pl.BlockSpec((bm, attn_o.shape[1]), row),
            pl.BlockSpec((bm, conv_u.shape[1]), row),
            pl.BlockSpec((bm, 2 * d), row),
            pl.BlockSpec(wa.shape, const, **resident),
            pl.BlockSpec(wc.shape, const, **resident),
            pl.BlockSpec(wo.shape, const, **resident),
        ],
        out_specs=pl.BlockSpec((bm, d), row),
        out_shape=jax.ShapeDtypeStruct((t, d), F32),
        compiler_params=pltpu.CompilerParams(
            dimension_semantics=("parallel",), vmem_limit_bytes=BIG_VMEM_LIMIT),
        name="merge",
    )(x1, attn_o, conv_u, gates, wa, wc, wo)


def kernel(x, ffn1_norm_g, ffn1_w_gate, ffn1_w_up, ffn1_w_down, mix_norm_g, w_in, rel_bias_table, lambda_q1, lambda_k1, lambda_q2, lambda_k2, attn_head_norm_g, w_attn_branch, conv_dw_w, conv_dw_b, conv_ln_g, conv_ln_b, w_conv_branch, w_out, ffn2_norm_g, ffn2_w_gate, ffn2_w_up, ffn2_w_down, final_norm_g):
    batch, seq, d = x.shape
    depth = ffn1_w_gate.shape[0]
    attn_w = N_HEADS * HEAD_W
    conv_ch = conv_dw_w.shape[2]
    xt = x.reshape(batch * seq, d)
    row = lambda v: v.reshape(1, -1).astype(F32)

    for l in range(depth):
        lam_init = 0.8 - 0.6 * math.exp(-0.3 * l)
        last = l == depth - 1
        qk_cols = (0, 2 * attn_w)
        glu_cols = (3 * attn_w, 3 * attn_w + 2 * conv_ch)
        gate_cols = (glu_cols[1], w_in.shape[2])
        w_in_bf = w_in[l].astype(BF16)
        w_vt = w_in[l][:, 2 * attn_w:3 * attn_w].T.astype(BF16)

        x1, h2 = _ffn(xt, row(ffn1_norm_g[l]), ffn1_w_gate[l].astype(BF16),
                      ffn1_w_up[l].astype(BF16), ffn1_w_down[l].astype(BF16),
                      row(mix_norm_g[l]), final=False)

        qk = _mm(h2, w_in_bf, BF16, cols=qk_cols, tile_scales=(HEAD_DIM ** -0.5, LOG2E),
                 name="proj_qk")
        vt = _mm_nt(w_vt, h2, BF16, name="proj_vt")
        glu = _mm(h2, w_in_bf, F32, cols=glu_cols, name="proj_glu")
        conv_w = jnp.repeat(conv_dw_w[l].astype(F32), SUBLANES, axis=0)
        gates, conv_u = _gates_conv(h2, w_in_bf, glu, conv_w, row(conv_dw_b[l]),
                                    row(conv_ln_g[l]), row(conv_ln_b[l]), cols=gate_cols,
                                    seq=seq)

        later = (w_attn_branch[l], w_conv_branch[l], w_out[l],
                 ffn2_w_gate[l], ffn2_w_up[l], ffn2_w_down[l])
        lam_vecs = jnp.stack([lambda_q1[l], lambda_k1[l], lambda_q2[l], lambda_k2[l]]).astype(F32)
        attn_o, w_a, w_c, w_o, w2_gate, w2_up, w2_down = _attention(
            qk, vt, rel_bias_table.astype(F32), lam_vecs,
            attn_head_norm_g[l].reshape(HEAD_W, 1).astype(F32),
            batch=batch, seq=seq, lam_init=lam_init, casts=tuple(w.astype(F32) for w in later))
        x2 = _merge(x1, attn_o, conv_u, gates, w_a, w_c, w_o)

        xt = _ffn(x2, row(ffn2_norm_g[l]), w2_gate, w2_up, w2_down, row(final_norm_g),
                  final=last)[0]

    return xt.reshape(batch, seq, d)
```

```python
import functools
import math

import jax
import jax.numpy as jnp
from jax import lax
from jax.experimental import pallas as pl
from jax.experimental.pallas import tpu as pltpu

F32 = jnp.float32
BF16 = jnp.bfloat16

N_HEADS = 8
HEAD_DIM = 64
HEAD_W = 2 * HEAD_DIM
CONV_WIDTH = 31
NUM_BUCKETS = 32
MAX_EXACT = NUM_BUCKETS // 2
MAX_DISTANCE = 128
NEG_INF = -1e30
LOG2E = math.log2(math.e)
LANES = 128
BF16_ROWS = 16
SUBLANES = 8
MXU_N = 256
MXU_K = 256
MIB = 1024 * 1024
VMEM_LIMIT = 56 * MIB
FFN_VMEM_LIMIT = 60 * MIB

FFN_SPLIT = 2
ATT_BLK = 128
ATT_GROUP = 4
CONV_HALO = 32
CONV_SHIFT_ROWS = 64


def _rms(x, g, eps):
    return x * lax.rsqrt(jnp.mean(x * x, axis=-1, keepdims=True) + eps) * g


def _sigmoid(x):
    return 1.0 / (1.0 + jnp.exp(-x))


def _ffn_body(x_ref, g_ref, wg_ref, wu_ref, wd_ref, ng_ref, *rest, final):
    if final:
        o_ref, h_ref = rest
    else:
        o_ref, hn_ref = rest
        h_ref = hn_ref
    f = pl.program_id(1)

    @pl.when(f == 0)
    def _():
        x = x_ref[...]
        h_ref[...] = _rms(x, g_ref[...], 1e-6).astype(BF16)
        o_ref[...] = x

    h = h_ref[...]
    tf = wg_ref.shape[1]
    halves = [slice(s * (tf // FFN_SPLIT), (s + 1) * (tf // FFN_SPLIT)) for s in range(FFN_SPLIT)]
    gu = [(jnp.dot(h, wg_ref[:, c], preferred_element_type=F32),
           jnp.dot(h, wu_ref[:, c], preferred_element_type=F32)) for c in halves]
    down = None
    for (gate, up), c in zip(gu, halves):
        act = ((gate * _sigmoid(gate)) * up * 0.5).astype(BF16)
        part = jnp.dot(act, wd_ref[c, :], preferred_element_type=F32)
        down = part if down is None else down + part
    o_ref[...] += down

    @pl.when(f == pl.num_programs(1) - 1)
    def _():
        y = _rms(o_ref[...], ng_ref[...], 1e-6)
        if final:
            o_ref[...] = y
        else:
            hn_ref[...] = y.astype(BF16)


def _ffn(x, g, wg, wu, wd, ng, *, final, bm=1024, tf=512):
    t, d = x.shape
    dff = wg.shape[1]
    grid = (t // bm, dff // tf)
    row = lambda i, f: (i, 0)
    out_shape = [jax.ShapeDtypeStruct((t, d), F32)]
    out_specs = [pl.BlockSpec((bm, d), row)]
    if not final:
        out_shape.append(jax.ShapeDtypeStruct((t, d), BF16))
        out_specs.append(pl.BlockSpec((bm, d), row))
    return pl.pallas_call(
        functools.partial(_ffn_body, final=final),
        grid=grid,
        in_specs=[
            pl.BlockSpec((bm, d), row),
            pl.BlockSpec((1, d), lambda i, f: (0, 0)),
            pl.BlockSpec((d, tf), lambda i, f: (0, f)),
            pl.BlockSpec((d, tf), lambda i, f: (0, f)),
            pl.BlockSpec((tf, d), lambda i, f: (f, 0)),
            pl.BlockSpec((1, d), lambda i, f: (0, 0)),
        ],
        out_specs=out_specs,
        out_shape=out_shape,
        scratch_shapes=[pltpu.VMEM((bm, d), BF16)] if final else [],
        compiler_params=pltpu.CompilerParams(
            dimension_semantics=("parallel", "arbitrary"), vmem_limit_bytes=FFN_VMEM_LIMIT),
        name="ffn_final" if final else "ffn",
    )(x, g, wg, wu, wd, ng)


def _mm_body(a_ref, b_ref, o_ref, *, tile_scales):
    acc = jnp.dot(a_ref[...], b_ref[...], preferred_element_type=F32)
    if tile_scales is not None:
        scale = jnp.float32(tile_scales[-1])
        for j, sc in enumerate(tile_scales[:-1]):
            scale = jnp.where(pl.program_id(1) == j, jnp.float32(sc), scale)
        acc = acc * scale
    o_ref[...] = acc.astype(o_ref.dtype)


def _mm(a, b, out_dtype, *, cols, bm=1024, tn=1024, tile_scales=None, name):
    m, k = a.shape
    n = cols[1] - cols[0]
    j0 = cols[0] // tn
    assert cols[0] % tn == 0 and n % tn == 0
    assert tile_scales is None or len(tile_scales) == n // tn
    return pl.pallas_call(
        functools.partial(_mm_body, tile_scales=tile_scales),
        grid=(m // bm, n // tn),
        in_specs=[pl.BlockSpec((bm, k), lambda i, j: (i, 0)),
                  pl.BlockSpec((k, tn), lambda i, j: (0, j0 + j))],
        out_specs=pl.BlockSpec((bm, tn), lambda i, j: (i, j)),
        out_shape=jax.ShapeDtypeStruct((m, n), out_dtype),
        compiler_params=pltpu.CompilerParams(
            dimension_semantics=("parallel", "arbitrary"), vmem_limit_bytes=VMEM_LIMIT),
        name=name,
    )(a, b)


def _mm_nt_body(wt_ref, a_ref, o_ref):
    o_ref[...] = lax.dot_general(wt_ref[...], a_ref[...], (((1,), (1,)), ((), ())),
                                 preferred_element_type=F32).astype(o_ref.dtype)


def _mm_nt(wt, a, out_dtype, *, bm=1024, name):
    n, k = wt.shape
    m = a.shape[0]
    return pl.pallas_call(
        _mm_nt_body,
        grid=(m // bm,),
        in_specs=[pl.BlockSpec((n, k), lambda i: (0, 0)),
                  pl.BlockSpec((bm, k), lambda i: (i, 0))],
        out_specs=pl.BlockSpec((n, bm), lambda i: (0, i)),
        out_shape=jax.ShapeDtypeStruct((n, m), out_dtype),
        compiler_params=pltpu.CompilerParams(
            dimension_semantics=("parallel",), vmem_limit_bytes=VMEM_LIMIT),
        name=name,
    )(wt, a)


def _t5_bias(d, tbl_ref, h):
    n = jnp.maximum(d, 0)
    nf = jnp.maximum(n, 1).astype(F32)
    large = MAX_EXACT + (jnp.log(nf / MAX_EXACT) / math.log(MAX_DISTANCE / MAX_EXACT)
                         * (NUM_BUCKETS - MAX_EXACT)).astype(jnp.int32)
    large = jnp.minimum(large, NUM_BUCKETS - 1)
    bucket = jnp.where(n < MAX_EXACT, n, large)
    b = jnp.zeros(d.shape, F32)
    for j in range(NUM_BUCKETS):
        b = jnp.where(bucket == j, tbl_ref[j, h], b)
    b = (b - tbl_ref[NUM_BUCKETS - 1, h]) * LOG2E
    return jnp.where(d >= 0, b, NEG_INF)


def _attn_body(tbl_ref, lam_ref, g_ref, q_ref, k_ref, vt_ref, *rest, lam_init, seq):
    n_cast = (len(rest) - 2) // 2
    cast_in, o_ref, cast_out, bias_ref = rest[:n_cast], rest[n_cast], rest[n_cast + 1:-1], rest[-1]
    for src, dst in zip(cast_in, cast_out):
        dst[...] = src[...].astype(BF16)

    h = pl.program_id(1)
    blk, grp = ATT_BLK, ATT_GROUP
    width = grp * blk
    nb = grp + 1
    kk = lax.broadcasted_iota(jnp.int32, (blk, blk), 0)
    jj = lax.broadcasted_iota(jnp.int32, (blk, blk), 1)
    diag = _t5_bias(jj - kk, tbl_ref, h)
    sub = _t5_bias(blk + jj - kk, tbl_ref, h)
    bias_ref[:(nb - 2) * blk, :] = jnp.zeros(((nb - 2) * blk, 2 * blk), F32)
    bias_ref[(nb - 2) * blk:(nb - 1) * blk, :] = jnp.concatenate([sub, sub], axis=1)
    bias_ref[(nb - 1) * blk:, :] = jnp.concatenate([diag, diag], axis=1)

    lv = lam_ref[...]
    lam = (jnp.exp(jnp.sum(lv[0:1] * lv[1:2], axis=-1, keepdims=True))
           - jnp.exp(jnp.sum(lv[2:3] * lv[3:4], axis=-1, keepdims=True)) + lam_init)
    g = g_ref[...]
    lane = lax.broadcasted_iota(jnp.int32, (blk, HEAD_W), 1)

    def load_qs(q0):
        out = []
        for r in range(grp):
            q = q_ref[pl.ds(q0 + r * blk, blk), :].astype(F32)
            out.append(jnp.concatenate([jnp.where(lane < HEAD_DIM, q, 0.0),
                                        jnp.where(lane >= HEAD_DIM, q, 0.0)],
                                       axis=0).astype(BF16))
        return out

    def scores(kb, qs, bias):
        st = lax.dot_general(kb, qs, (((1,), (1,)), ((), ())), preferred_element_type=F32)
        return st if bias is None else st + bias

    def update(state, st, vt):
        m, acc = state
        m_new = jnp.maximum(m, jnp.max(st, axis=0, keepdims=True))
        alpha = jnp.exp2(m - m_new)
        p = jnp.exp2(st - m_new)
        acc = alpha * acc + jnp.dot(vt, p.astype(BF16), preferred_element_type=F32)
        return m_new, acc

    def finish(state, q0):
        _, acc = state
        o = acc[:HEAD_W] * (1.0 / acc[HEAD_W:HEAD_W + 1])
        ot = o[:, :blk] - lam * o[:, blk:]
        y = ot * lax.rsqrt(jnp.mean(ot * ot, axis=0, keepdims=True) + 1e-5) * g
        y = y * (1.0 - lam_init)
        o_ref[pl.ds(q0, blk), :] = y.T.astype(o_ref.dtype)

    init = ((jnp.full((1, 2 * blk), NEG_INF, F32),
             jnp.zeros((HEAD_W + BF16_ROWS, 2 * blk), F32)),) * grp
    ones_rows = jnp.ones((BF16_ROWS, width), BF16)

    steps = []
    for gi in range(seq // width):
        kinds = ["far"] * max(gi - 1, 0) + (["prev"] if gi else []) + ["own"]
        for n, kind in enumerate(kinds):
            steps.append((gi, kind, n * width, n == len(kinds) - 1))

    def chain_scores(step, qs, r):
        gi, kind, k0, _ = step
        if kind == "far":
            return scores(k_ref[pl.ds(k0, width), :], qs[r], None)
        if kind == "prev":
            return scores(k_ref[pl.ds(k0, width), :], qs[r], bias_ref[:width, :] if r == 0 else None)
        return scores(k_ref[pl.ds(k0, (r + 1) * blk), :], qs[r], bias_ref[(grp - r) * blk:, :])

    qs = load_qs(0)
    cur = [chain_scores(steps[0], qs, r) for r in range(grp)]
    states = list(init)
    for n, step in enumerate(steps):
        gi, _, k0, last = step
        more = n + 1 < len(steps)
        if more and last:
            qs = load_qs(steps[n + 1][0] * width)
        vt = jnp.concatenate([vt_ref[:, pl.ds(k0, width)], ones_rows], axis=0)
        nxt = []
        for r in range(grp):
            if more:
                nxt.append(chain_scores(steps[n + 1], qs, r))
            states[r] = update(states[r], cur[r], vt[:, :cur[r].shape[0]])
        if last:
            for r in range(grp):
                finish(states[r], gi * width + r * blk)
            states = list(init)
        cur = nxt


def _attention(qk, vt, tbl, lam_vecs, g_col, *, batch, seq, lam_init, casts=()):
    t = qk.shape[0]
    n_steps = batch * N_HEADS
    slab = lambda b, h: (b * N_HEADS + h, 0)
    for w in casts:
        assert w.shape[0] % (n_steps * BF16_ROWS) == 0, w.shape
    cast_specs = [pl.BlockSpec((w.shape[0] // n_steps, w.shape[1]), slab) for w in casts]
    return pl.pallas_call(
        functools.partial(_attn_body, lam_init=lam_init, seq=seq),
        grid=(batch, N_HEADS),
        in_specs=[
            pl.BlockSpec(memory_space=pltpu.SMEM),
            pl.BlockSpec((4, HEAD_DIM), lambda b, h: (0, 0)),
            pl.BlockSpec((HEAD_W, 1), lambda b, h: (0, 0)),
            pl.BlockSpec((seq, HEAD_W), lambda b, h: (b, h)),
            pl.BlockSpec((seq, HEAD_W), lambda b, h: (b, N_HEADS + h)),
            pl.BlockSpec((HEAD_W, seq), lambda b, h: (h, b)),
        ] + cast_specs,
        out_specs=[pl.BlockSpec((seq, HEAD_W), lambda b, h: (b, h))] + cast_specs,
        out_shape=[jax.ShapeDtypeStruct((t, N_HEADS * HEAD_W), BF16)]
        + [jax.ShapeDtypeStruct(w.shape, BF16) for w in casts],
        scratch_shapes=[pltpu.VMEM(((ATT_GROUP + 1) * ATT_BLK, 2 * ATT_BLK), F32)],
        compiler_params=pltpu.CompilerParams(
            dimension_semantics=("parallel", "parallel"), vmem_limit_bytes=VMEM_LIMIT),
        name="diff_attn",
    )(tbl, lam_vecs, g_col, qk, qk, vt, *casts)


def _conv_prepare(first, cur_ref, halo_ref, ext_ref, bs, c):
    cur = cur_ref[...]
    ext_ref[0, CONV_HALO:, :] = cur[:, :c] * _sigmoid(cur[:, c:])
    hal = halo_ref[...]
    uh = hal[:, :c] * _sigmoid(hal[:, c:])
    ext_ref[0, :CONV_HALO, :] = jnp.where(first, 0.0, uh)

    def shift_rows(r0, n):
        win = ext_ref[0, r0:r0 + n + SUBLANES, :]
        for s in range(1, SUBLANES):
            ext_ref[s, r0:r0 + n, :] = win[s:s + n]

    for r0 in range(0, bs, CONV_SHIFT_ROWS):
        shift_rows(r0, CONV_SHIFT_ROWS)
    shift_rows(bs, CONV_HALO - SUBLANES)


def _conv_group(r0, w_ref, b_ref, g_ref, beta_ref, o_ref, ext_ref):
    acc = jnp.zeros((SUBLANES, o_ref.shape[1]), F32)
    for j in range(CONV_WIDTH):
        off = CONV_HALO - (CONV_WIDTH - 1) + j
        s, a = off % SUBLANES, off // SUBLANES
        start = r0 + a * SUBLANES
        acc = acc + w_ref[j * SUBLANES:(j + 1) * SUBLANES, :] * ext_ref[s, start:start + SUBLANES, :]
    y = acc + b_ref[...]
    mu = jnp.mean(y, axis=-1, keepdims=True)
    var = jnp.mean(jnp.square(y - mu), axis=-1, keepdims=True)
    yn = (y - mu) * lax.rsqrt(var + 1e-5) * g_ref[...] + beta_ref[...]
    out = yn * _sigmoid(yn)
    o_ref[r0:r0 + SUBLANES, :] = out.astype(o_ref.dtype)
    return out


def _zero_after(x):
    bits = pltpu.bitcast(jnp.concatenate([x[:SUBLANES, :LANES]] * 2, axis=0), jnp.uint32)
    bits = lax.shift_right_logical(lax.shift_right_logical(bits, jnp.uint32(16)), jnp.uint32(16))
    return pltpu.bitcast(bits, F32).astype(BF16)


def _gates_conv_body(h_ref, wg_ref, cur_ref, halo_ref, w_ref, b_ref, g_ref, beta_ref,
                     gate_o_ref, conv_o_ref, ext_ref, *, blocks_per_seq):
    bs, c = conv_o_ref.shape
    tn = wg_ref.shape[1]
    blk = pl.program_id(0) * pl.num_programs(1) + pl.program_id(1)
    _conv_prepare(blk % blocks_per_seq == 0, cur_ref, halo_ref, ext_ref, bs, c)
    n_chunks = tn // MXU_N
    k_pieces = h_ref.shape[1] // MXU_K
    per_piece = (bs // SUBLANES) // (n_chunks * k_pieces)
    anchor = None
    r0 = 0
    for ci in range(n_chunks):
        cols = slice(ci * MXU_N, (ci + 1) * MXU_N)
        acc = None
        for ki in range(k_pieces):
            rows = slice(ki * MXU_K, (ki + 1) * MXU_K)
            w = wg_ref[rows, cols]
            if anchor is not None:
                top = jnp.concatenate([w[:BF16_ROWS, :LANES] + anchor, w[:BF16_ROWS, LANES:]],
                                      axis=1)
                w = jnp.concatenate(
                    [top, wg_ref[ki * MXU_K + BF16_ROWS:(ki + 1) * MXU_K, cols]], axis=0)
            part = jnp.dot(h_ref[:, rows], w, preferred_element_type=F32)
            acc = part if acc is None else acc + part
            for _ in range(per_piece):
                out = _conv_group(r0, w_ref, b_ref, g_ref, beta_ref, conv_o_ref, ext_ref)
                r0 += SUBLANES
            anchor = _zero_after(out)
        gate_o_ref[:, cols] = acc
    assert r0 == bs


def _gates_conv(h2, w_in, glu, w, b, g, beta, *, cols, seq, bm=1024, tn=1024):
    t, k = h2.shape
    n = cols[1] - cols[0]
    j0 = cols[0] // tn
    assert cols[0] % tn == 0 and n % tn == 0
    c2 = glu.shape[1]
    c = c2 // 2
    steps = n // tn
    bs = bm // steps
    per = bs // CONV_HALO
    vec = lambda i, j: (0, 0)
    return pl.pallas_call(
        functools.partial(_gates_conv_body, blocks_per_seq=seq // bs),
        grid=(t // bm, steps),
        in_specs=[
            pl.BlockSpec((bm, k), lambda i, j: (i, 0)),
            pl.BlockSpec((k, tn), lambda i, j: (0, j0 + j)),
            pl.BlockSpec((bs, c2), lambda i, j: (i * steps + j, 0)),
            pl.BlockSpec((CONV_HALO, c2),
                         lambda i, j: (jnp.maximum((i * steps + j) * per - 1, 0), 0)),
            pl.BlockSpec((CONV_WIDTH * SUBLANES, c), vec),
            pl.BlockSpec((1, c), vec),
            pl.BlockSpec((1, c), vec),
            pl.BlockSpec((1, c), vec),
        ],
        out_specs=[pl.BlockSpec((bm, tn), lambda i, j: (i, j)),
                   pl.BlockSpec((bs, c), lambda i, j: (i * steps + j, 0))],
        out_shape=[jax.ShapeDtypeStruct((t, n), F32), jax.ShapeDtypeStruct((t, c), BF16)],
        scratch_shapes=[pltpu.VMEM((SUBLANES, CONV_HALO + bs, c), F32)],
        compiler_params=pltpu.CompilerParams(
            dimension_semantics=("parallel", "arbitrary"), vmem_limit_bytes=VMEM_LIMIT),
        name="gates_conv",
    )(h2, w_in, glu, glu, w, b, g, beta)


def _merge_body(x_ref, a_ref, u_ref, gate_ref, wa_ref, wc_ref, wo_ref, o_ref):
    d = o_ref.shape[1]
    ab = jnp.dot(a_ref[...], wa_ref[...], preferred_element_type=F32)
    cb = jnp.dot(u_ref[...], wc_ref[...], preferred_element_type=F32)
    gates = gate_ref[...]
    merged = _sigmoid(gates[:, :d]) * ab + _sigmoid(gates[:, d:]) * cb
    o_ref[...] = x_ref[...] + jnp.dot(merged.astype(BF16), wo_ref[...],
                                      preferred_element_type=F32)


def _merge(x1, attn_o, conv_u, gates, wa, wc, wo, *, bm=256):
    t, d = x1.shape
    row = lambda i: (i, 0)
    const = lambda i: (0, 0)
    resident = dict(pipeline_mode=pl.Buffered(1))
    return pl.pallas_call(
        _merge_body,
        grid=(t // bm,),
        in_specs=[
            pl.BlockSpec((bm, d), row),
            pl.BlockSpec((bm, attn_o.shape[1]), row),
            pl.BlockSpec((bm, conv_u.shape[1]), row),
            pl.BlockSpec((bm, 2 * d), row),
            pl.BlockSpec(wa.shape, const, **resident),
            pl.BlockSpec(wc.shape, const, **resident),
            pl.BlockSpec(wo.shape, const, **resident),
        ],
        out_specs=pl.BlockSpec((bm, d), row),
        out_shape=jax.ShapeDtypeStruct((t, d), F32),
        compiler_params=pltpu.CompilerParams(
            dimension_semantics=("parallel",), vmem_limit_bytes=VMEM_LIMIT),
        name="merge",
    )(x1, attn_o, conv_u, gates, wa, wc, wo)


def kernel(x, ffn1_norm_g, ffn1_w_gate, ffn1_w_up, ffn1_w_down, mix_norm_g, w_in, rel_bias_table, lambda_q1, lambda_k1, lambda_q2, lambda_k2, attn_head_norm_g, w_attn_branch, conv_dw_w, conv_dw_b, conv_ln_g, conv_ln_b, w_conv_branch, w_out, ffn2_norm_g, ffn2_w_gate, ffn2_w_up, ffn2_w_down, final_norm_g):
    batch, seq, d = x.shape
    depth = ffn1_w_gate.shape[0]
    attn_w = N_HEADS * HEAD_W
    conv_ch = conv_dw_w.shape[2]
    xt = x.reshape(batch * seq, d)
    row = lambda v: v.reshape(1, -1).astype(F32)

    for l in range(depth):
        lam_init = 0.8 - 0.6 * math.exp(-0.3 * l)
        last = l == depth - 1
        qk_cols = (0, 2 * attn_w)
        glu_cols = (3 * attn_w, 3 * attn_w + 2 * conv_ch)
        gate_cols = (glu_cols[1], w_in.shape[2])
        w_in_bf = w_in[l].astype(BF16)
        w_vt = w_in[l][:, 2 * attn_w:3 * attn_w].T.astype(BF16)

        x1, h2 = _ffn(xt, row(ffn1_norm_g[l]), ffn1_w_gate[l].astype(BF16),
                      ffn1_w_up[l].astype(BF16), ffn1_w_down[l].astype(BF16),
                      row(mix_norm_g[l]), final=False)

        qk = _mm(h2, w_in_bf, BF16, cols=qk_cols, tile_scales=(HEAD_DIM ** -0.5, LOG2E),
                 name="proj_qk")
        vt = _mm_nt(w_vt, h2, BF16, name="proj_vt")
        glu = _mm(h2, w_in_bf, F32, cols=glu_cols, name="proj_glu")
        conv_w = jnp.repeat(conv_dw_w[l].astype(F32), SUBLANES, axis=0)
        gates, conv_u = _gates_conv(h2, w_in_bf, glu, conv_w, row(conv_dw_b[l]),
                                    row(conv_ln_g[l]), row(conv_ln_b[l]), cols=gate_cols,
                                    seq=seq)

        later = (w_attn_branch[l], w_conv_branch[l], w_out[l],
                 ffn2_w_gate[l], ffn2_w_up[l], ffn2_w_down[l])
        lam_vecs = jnp.stack([lambda_q1[l], lambda_k1[l], lambda_q2[l], lambda_k2[l]]).astype(F32)
        attn_o, w_a, w_c, w_o, w2_gate, w2_up, w2_down = _attention(
            qk, vt, rel_bias_table.astype(F32), lam_vecs,
            attn_head_norm_g[l].reshape(HEAD_W, 1).astype(F32),
            batch=batch, seq=seq, lam_init=lam_init, casts=tuple(w.astype(F32) for w in later))
        x2 = _merge(x1, attn_o, conv_u, gates, w_a, w_c, w_o)

        xt = _ffn(x2, row(ffn2_norm_g[l]), w2_gate, w2_up, w2_down, row(final_norm_g),
                  final=last)[0]

    return xt.reshape(batch, seq, d)
```

```python
import functools
import math

import jax
import jax.numpy as jnp
from jax import lax
from jax.experimental import pallas as pl
from jax.experimental.pallas import tpu as pltpu

F32 = jnp.float32
BF16 = jnp.bfloat16

N_HEADS = 8
HEAD_DIM = 64
HEAD_W = 2 * HEAD_DIM
CONV_WIDTH = 31
NUM_BUCKETS = 32
MAX_EXACT = NUM_BUCKETS // 2
MAX_DISTANCE = 128
NEG_INF = -1e30
LOG2E = math.log2(math.e)
LANES = 128
BF16_ROWS = 16
SUBLANES = 8
MXU_N = 256
MXU_K = 256
MIB = 1024 * 1024
VMEM_LIMIT = 56 * MIB
FFN_VMEM_LIMIT = 60 * MIB

FFN_SPLIT = 2
ATT_BLK = 128
PV_KEYS = 256
ATT_GROUP = 4
CONV_HALO = 32
CONV_SHIFT_ROWS = 64


def _rms(x, g, eps):
    return x * lax.rsqrt(jnp.mean(x * x, axis=-1, keepdims=True) + eps) * g


def _sigmoid(x):
    return 1.0 / (1.0 + jnp.exp(-x))


def _ffn_body(x_ref, g_ref, wg_ref, wu_ref, wd_ref, ng_ref, *rest, final):
    if final:
        o_ref, h_ref = rest
    else:
        o_ref, hn_ref = rest
        h_ref = hn_ref
    f = pl.program_id(1)

    @pl.when(f == 0)
    def _():
        x = x_ref[...]
        h_ref[...] = _rms(x, g_ref[...], 1e-6).astype(BF16)
        o_ref[...] = x

    h = h_ref[...]
    tf = wg_ref.shape[1]
    halves = [slice(s * (tf // FFN_SPLIT), (s + 1) * (tf // FFN_SPLIT)) for s in range(FFN_SPLIT)]
    gu = [(jnp.dot(h, wg_ref[:, c], preferred_element_type=F32),
           jnp.dot(h, wu_ref[:, c], preferred_element_type=F32)) for c in halves]
    down = None
    for (gate, up), c in zip(gu, halves):
        act = ((gate * _sigmoid(gate)) * up * 0.5).astype(BF16)
        part = jnp.dot(act, wd_ref[c, :], preferred_element_type=F32)
        down = part if down is None else down + part
    o_ref[...] += down

    @pl.when(f == pl.num_programs(1) - 1)
    def _():
        y = _rms(o_ref[...], ng_ref[...], 1e-6)
        if final:
            o_ref[...] = y
        else:
            hn_ref[...] = y.astype(BF16)


def _ffn(x, g, wg, wu, wd, ng, *, final, bm=1024, tf=512):
    t, d = x.shape
    dff = wg.shape[1]
    grid = (t // bm, dff // tf)
    row = lambda i, f: (i, 0)
    out_shape = [jax.ShapeDtypeStruct((t, d), F32)]
    out_specs = [pl.BlockSpec((bm, d), row)]
    if not final:
        out_shape.append(jax.ShapeDtypeStruct((t, d), BF16))
        out_specs.append(pl.BlockSpec((bm, d), row))
    return pl.pallas_call(
        functools.partial(_ffn_body, final=final),
        grid=grid,
        in_specs=[
            pl.BlockSpec((bm, d), row),
            pl.BlockSpec((1, d), lambda i, f: (0, 0)),
            pl.BlockSpec((d, tf), lambda i, f: (0, f)),
            pl.BlockSpec((d, tf), lambda i, f: (0, f)),
            pl.BlockSpec((tf, d), lambda i, f: (f, 0)),
            pl.BlockSpec((1, d), lambda i, f: (0, 0)),
        ],
        out_specs=out_specs,
        out_shape=out_shape,
        scratch_shapes=[pltpu.VMEM((bm, d), BF16)] if final else [],
        compiler_params=pltpu.CompilerParams(
            dimension_semantics=("parallel", "arbitrary"), vmem_limit_bytes=FFN_VMEM_LIMIT),
        name="ffn_final" if final else "ffn",
    )(x, g, wg, wu, wd, ng)


def _mm_body(a_ref, b_ref, o_ref, *, tile_scales):
    acc = jnp.dot(a_ref[...], b_ref[...], preferred_element_type=F32)
    if tile_scales is not None:
        scale = jnp.float32(tile_scales[-1])
        for j, sc in enumerate(tile_scales[:-1]):
            scale = jnp.where(pl.program_id(1) == j, jnp.float32(sc), scale)
        acc = acc * scale
    o_ref[...] = acc.astype(o_ref.dtype)


def _mm(a, b, out_dtype, *, cols, bm=2048, tn=1024, tile_scales=None, name):
    m, k = a.shape
    n = cols[1] - cols[0]
    j0 = cols[0] // tn
    assert cols[0] % tn == 0 and n % tn == 0
    assert tile_scales is None or len(tile_scales) == n // tn
    return pl.pallas_call(
        functools.partial(_mm_body, tile_scales=tile_scales),
        grid=(m // bm, n // tn),
        in_specs=[pl.BlockSpec((bm, k), lambda i, j: (i, 0)),
                  pl.BlockSpec((k, tn), lambda i, j: (0, j0 + j))],
        out_specs=pl.BlockSpec((bm, tn), lambda i, j: (i, j)),
        out_shape=jax.ShapeDtypeStruct((m, n), out_dtype),
        compiler_params=pltpu.CompilerParams(
            dimension_semantics=("parallel", "arbitrary"), vmem_limit_bytes=VMEM_LIMIT),
        name=name,
    )(a, b)


def _mm_nt_body(wt_ref, a_ref, o_ref):
    o_ref[...] = lax.dot_general(wt_ref[...], a_ref[...], (((1,), (1,)), ((), ())),
                                 preferred_element_type=F32).astype(o_ref.dtype)


def _mm_nt(wt, a, out_dtype, *, bm=2048, name):
    n, k = wt.shape
    m = a.shape[0]
    return pl.pallas_call(
        _mm_nt_body,
        grid=(m // bm,),
        in_specs=[pl.BlockSpec((n, k), lambda i: (0, 0)),
                  pl.BlockSpec((bm, k), lambda i: (i, 0))],
        out_specs=pl.BlockSpec((n, bm), lambda i: (0, i)),
        out_shape=jax.ShapeDtypeStruct((n, m), out_dtype),
        compiler_params=pltpu.CompilerParams(
            dimension_semantics=("parallel",), vmem_limit_bytes=VMEM_LIMIT),
        name=name,
    )(wt, a)


def _t5_bias(d, tbl_ref, h):
    n = jnp.maximum(d, 0)
    nf = jnp.maximum(n, 1).astype(F32)
    large = MAX_EXACT + (jnp.log(nf / MAX_EXACT) / math.log(MAX_DISTANCE / MAX_EXACT)
                         * (NUM_BUCKETS - MAX_EXACT)).astype(jnp.int32)
    large = jnp.minimum(large, NUM_BUCKETS - 1)
    bucket = jnp.where(n < MAX_EXACT, n, large)
    b = jnp.zeros(d.shape, F32)
    for j in range(NUM_BUCKETS):
        b = jnp.where(bucket == j, tbl_ref[j, h], b)
    b = (b - tbl_ref[NUM_BUCKETS - 1, h]) * LOG2E
    return jnp.where(d >= 0, b, NEG_INF)


def _attn_body(tbl_ref, lam_ref, g_ref, q_ref, k_ref, vt_ref, *rest, lam_init, seq):
    n_cast = (len(rest) - 2) // 2
    cast_in, o_ref, cast_out, bias_ref = rest[:n_cast], rest[n_cast], rest[n_cast + 1:-1], rest[-1]
    for src, dst in zip(cast_in, cast_out):
        dst[...] = src[...].astype(BF16)

    h = pl.program_id(1)
    blk, grp = ATT_BLK, ATT_GROUP
    width = grp * blk
    nb = grp + 1
    kk = lax.broadcasted_iota(jnp.int32, (blk, blk), 0)
    jj = lax.broadcasted_iota(jnp.int32, (blk, blk), 1)
    diag = _t5_bias(jj - kk, tbl_ref, h)
    sub = _t5_bias(blk + jj - kk, tbl_ref, h)
    bias_ref[:(nb - 2) * blk, :] = jnp.zeros(((nb - 2) * blk, 2 * blk), F32)
    bias_ref[(nb - 2) * blk:(nb - 1) * blk, :] = jnp.concatenate([sub, sub], axis=1)
    bias_ref[(nb - 1) * blk:, :] = jnp.concatenate([diag, diag], axis=1)

    lv = lam_ref[...]
    lam = (jnp.exp(jnp.sum(lv[0:1] * lv[1:2], axis=-1, keepdims=True))
           - jnp.exp(jnp.sum(lv[2:3] * lv[3:4], axis=-1, keepdims=True)) + lam_init)
    g = g_ref[...]
    lane = lax.broadcasted_iota(jnp.int32, (blk, HEAD_W), 1)

    def load_qs(q0):
        out = []
        for r in range(grp):
            q = q_ref[pl.ds(q0 + r * blk, blk), :].astype(F32)
            out.append(jnp.concatenate([jnp.where(lane < HEAD_DIM, q, 0.0),
                                        jnp.where(lane >= HEAD_DIM, q, 0.0)],
                                       axis=0).astype(BF16))
        return out

    def scores(kb, qs, bias):
        st = lax.dot_general(kb, qs, (((1,), (1,)), ((), ())), preferred_element_type=F32)
        return st if bias is None else st + bias

    def update(state, st, vt):
        m, acc = state
        m_new = jnp.maximum(m, jnp.max(st, axis=0, keepdims=True))
        acc = jnp.exp2(m - m_new) * acc
        n = st.shape[0]
        for k0 in range(0, n, PV_KEYS):
            k1 = min(k0 + PV_KEYS, n)
            p = jnp.exp2(st[k0:k1] - m_new).astype(BF16)
            acc = acc + jnp.dot(vt[:, k0:k1], p, preferred_element_type=F32)
        return m_new, acc

    def finish(state, q0):
        _, acc = state
        o = acc[:HEAD_W] * (1.0 / acc[HEAD_W:HEAD_W + 1])
        ot = o[:, :blk] - lam * o[:, blk:]
        y = ot * lax.rsqrt(jnp.mean(ot * ot, axis=0, keepdims=True) + 1e-5) * g
        y = y * (1.0 - lam_init)
        o_ref[pl.ds(q0, blk), :] = y.T.astype(o_ref.dtype)

    init = ((jnp.full((1, 2 * blk), NEG_INF, F32),
             jnp.zeros((HEAD_W + BF16_ROWS, 2 * blk), F32)),) * grp
    ones_rows = jnp.ones((BF16_ROWS, width), BF16)

    steps = []
    for gi in range(seq // width):
        kinds = ["far"] * max(gi - 1, 0) + (["prev"] if gi else []) + ["own"]
        for n, kind in enumerate(kinds):
            steps.append((gi, kind, n * width, n == len(kinds) - 1))

    def chain_scores(step, qs, r):
        gi, kind, k0, _ = step
        if kind == "far":
            return scores(k_ref[pl.ds(k0, width), :], qs[r], None)
        if kind == "prev":
            return scores(k_ref[pl.ds(k0, width), :], qs[r], bias_ref[:width, :] if r == 0 else None)
        return scores(k_ref[pl.ds(k0, (r + 1) * blk), :], qs[r], bias_ref[(grp - r) * blk:, :])

    qs = load_qs(0)
    cur = [chain_scores(steps[0], qs, r) for r in range(grp)]
    states = list(init)
    for n, step in enumerate(steps):
        gi, _, k0, last = step
        more = n + 1 < len(steps)
        if more and last:
            qs = load_qs(steps[n + 1][0] * width)
        vt = jnp.concatenate([vt_ref[:, pl.ds(k0, width)], ones_rows], axis=0)
        nxt = []
        for r in range(grp):
            states[r] = update(states[r], cur[r], vt[:, :cur[r].shape[0]])
            if more:
                nxt.append(chain_scores(steps[n + 1], qs, r))
        if last:
            for r in range(grp):
                finish(states[r], gi * width + r * blk)
            states = list(init)
        cur = nxt


def _attention(qk, vt, tbl, lam_vecs, g_col, *, batch, seq, lam_init, casts=()):
    t = qk.shape[0]
    n_steps = batch * N_HEADS
    slab = lambda b, h: (b * N_HEADS + h, 0)
    for w in casts:
        assert w.shape[0] % (n_steps * BF16_ROWS) == 0, w.shape
    cast_specs = [pl.BlockSpec((w.shape[0] // n_steps, w.shape[1]), slab) for w in casts]
    return pl.pallas_call(
        functools.partial(_attn_body, lam_init=lam_init, seq=seq),
        grid=(batch, N_HEADS),
        in_specs=[
            pl.BlockSpec(memory_space=pltpu.SMEM),
            pl.BlockSpec((4, HEAD_DIM), lambda b, h: (0, 0)),
            pl.BlockSpec((HEAD_W, 1), lambda b, h: (0, 0)),
            pl.BlockSpec((seq, HEAD_W), lambda b, h: (b, h)),
            pl.BlockSpec((seq, HEAD_W), lambda b, h: (b, N_HEADS + h)),
            pl.BlockSpec((HEAD_W, seq), lambda b, h: (h, b)),
        ] + cast_specs,
        out_specs=[pl.BlockSpec((seq, HEAD_W), lambda b, h: (b, h))] + cast_specs,
        out_shape=[jax.ShapeDtypeStruct((t, N_HEADS * HEAD_W), BF16)]
        + [jax.ShapeDtypeStruct(w.shape, BF16) for w in casts],
        scratch_shapes=[pltpu.VMEM(((ATT_GROUP + 1) * ATT_BLK, 2 * ATT_BLK), F32)],
        compiler_params=pltpu.CompilerParams(
            dimension_semantics=("parallel", "parallel"), vmem_limit_bytes=VMEM_LIMIT),
        name="diff_attn",
    )(tbl, lam_vecs, g_col, qk, qk, vt, *casts)


def _conv_prepare(first, cur_ref, halo_ref, ext_ref, bs, c):
    cur = cur_ref[...]
    ext_ref[0, CONV_HALO:, :] = cur[:, :c] * _sigmoid(cur[:, c:])
    hal = halo_ref[...]
    uh = hal[:, :c] * _sigmoid(hal[:, c:])
    ext_ref[0, :CONV_HALO, :] = jnp.where(first, 0.0, uh)

    def shift_rows(r0, n):
        win = ext_ref[0, r0:r0 + n + SUBLANES, :]
        for s in range(1, SUBLANES):
            ext_ref[s, r0:r0 + n, :] = win[s:s + n]

    for r0 in range(0, bs, CONV_SHIFT_ROWS):
        shift_rows(r0, CONV_SHIFT_ROWS)
    shift_rows(bs, CONV_HALO - SUBLANES)


def _conv_group(r0, w_ref, b_ref, g_ref, beta_ref, o_ref, ext_ref):
    acc = jnp.zeros((SUBLANES, o_ref.shape[1]), F32)
    for j in range(CONV_WIDTH):
        off = CONV_HALO - (CONV_WIDTH - 1) + j
        s, a = off % SUBLANES, off // SUBLANES
        start = r0 + a * SUBLANES
        acc = acc + w_ref[j * SUBLANES:(j + 1) * SUBLANES, :] * ext_ref[s, start:start + SUBLANES, :]
    y = acc + b_ref[...]
    mu = jnp.mean(y, axis=-1, keepdims=True)
    var = jnp.mean(jnp.square(y - mu), axis=-1, keepdims=True)
    yn = (y - mu) * lax.rsqrt(var + 1e-5) * g_ref[...] + beta_ref[...]
    out = yn * _sigmoid(yn)
    o_ref[r0:r0 + SUBLANES, :] = out.astype(o_ref.dtype)
    return out


def _zero_after(x):
    bits = pltpu.bitcast(jnp.concatenate([x[:SUBLANES, :LANES]] * 2, axis=0), jnp.uint32)
    bits = lax.shift_right_logical(lax.shift_right_logical(bits, jnp.uint32(16)), jnp.uint32(16))
    return pltpu.bitcast(bits, F32).astype(BF16)


def _gates_conv_body(h_ref, wg_ref, cur_ref, halo_ref, w_ref, b_ref, g_ref, beta_ref,
                     gate_o_ref, conv_o_ref, ext_ref, *, blocks_per_seq):
    bs, c = conv_o_ref.shape
    tn = wg_ref.shape[1]
    blk = pl.program_id(0) * pl.num_programs(1) + pl.program_id(1)
    _conv_prepare(blk % blocks_per_seq == 0, cur_ref, halo_ref, ext_ref, bs, c)
    n_chunks = tn // MXU_N
    k_pieces = h_ref.shape[1] // MXU_K
    per_piece = (bs // SUBLANES) // (n_chunks * k_pieces)
    anchor = None
    r0 = 0
    for ci in range(n_chunks):
        cols = slice(ci * MXU_N, (ci + 1) * MXU_N)
        acc = None
        for ki in range(k_pieces):
            rows = slice(ki * MXU_K, (ki + 1) * MXU_K)
            w = wg_ref[rows, cols]
            if anchor is not None:
                top = jnp.concatenate([w[:BF16_ROWS, :LANES] + anchor, w[:BF16_ROWS, LANES:]],
                                      axis=1)
                w = jnp.concatenate(
                    [top, wg_ref[ki * MXU_K + BF16_ROWS:(ki + 1) * MXU_K, cols]], axis=0)
            part = jnp.dot(h_ref[:, rows], w, preferred_element_type=F32)
            acc = part if acc is None else acc + part
            for _ in range(per_piece):
                out = _conv_group(r0, w_ref, b_ref, g_ref, beta_ref, conv_o_ref, ext_ref)
                r0 += SUBLANES
            anchor = _zero_after(out)
        gate_o_ref[:, cols] = acc
    assert r0 == bs


def _gates_conv(h2, w_in, glu, w, b, g, beta, *, cols, seq, bm=1024, tn=1024):
    t, k = h2.shape
    n = cols[1] - cols[0]
    j0 = cols[0] // tn
    assert cols[0] % tn == 0 and n % tn == 0
    c2 = glu.shape[1]
    c = c2 // 2
    steps = n // tn
    bs = bm // steps
    per = bs // CONV_HALO
    vec = lambda i, j: (0, 0)
    return pl.pallas_call(
        functools.partial(_gates_conv_body, blocks_per_seq=seq // bs),
        grid=(t // bm, steps),
        in_specs=[
            pl.BlockSpec((bm, k), lambda i, j: (i, 0)),
            pl.BlockSpec((k, tn), lambda i, j: (0, j0 + j)),
            pl.BlockSpec((bs, c2), lambda i, j: (i * steps + j, 0)),
            pl.BlockSpec((CONV_HALO, c2),
                         lambda i, j: (jnp.maximum((i * steps + j) * per - 1, 0), 0)),
            pl.BlockSpec((CONV_WIDTH * SUBLANES, c), vec),
            pl.BlockSpec((1, c), vec),
            pl.BlockSpec((1, c), vec),
            pl.BlockSpec((1, c), vec),
        ],
        out_specs=[pl.BlockSpec((bm, tn), lambda i, j: (i, j)),
                   pl.BlockSpec((bs, c), lambda i, j: (i * steps + j, 0))],
        out_shape=[jax.ShapeDtypeStruct((t, n), F32), jax.ShapeDtypeStruct((t, c), BF16)],
        scratch_shapes=[pltpu.VMEM((SUBLANES, CONV_HALO + bs, c), F32)],
        compiler_params=pltpu.CompilerParams(
            dimension_semantics=("parallel", "arbitrary"), vmem_limit_bytes=VMEM_LIMIT),
        name="gates_conv",
    )(h2, w_in, glu, glu, w, b, g, beta)


def _merge_body(x_ref, a_ref, u_ref, gate_ref, wa_ref, wc_ref, wo_ref, o_ref):
    d = o_ref.shape[1]
    ab = jnp.dot(a_ref[...], wa_ref[...], preferred_element_type=F32)
    cb = jnp.dot(u_ref[...], wc_ref[...], preferred_element_type=F32)
    gates = gate_ref[...]
    merged = _sigmoid(gates[:, :d]) * ab + _sigmoid(gates[:, d:]) * cb
    o_ref[...] = x_ref[...] + jnp.dot(merged.astype(BF16), wo_ref[...],
                                      preferred_element_type=F32)


def _merge(x1, attn_o, conv_u, gates, wa, wc, wo, *, bm=256):
    t, d = x1.shape
    row = lambda i: (i, 0)
    const = lambda i: (0, 0)
    resident = dict(pipeline_mode=pl.Buffered(1))
    return pl.pallas_call(
        _merge_body,
        grid=(t // bm,),
        in_specs=[
            pl.BlockSpec((bm, d), row),
            pl.BlockSpec((bm, attn_o.shape[1]), row),
            pl.BlockSpec((bm, conv_u.shape[1]), row),
            pl.BlockSpec((bm, 2 * d), row),
            pl.BlockSpec(wa.shape, const, **resident),
            pl.BlockSpec(wc.shape, const, **resident),
            pl.BlockSpec(wo.shape, const, **resident),
        ],
        out_specs=pl.BlockSpec((bm, d), row),
        out_shape=jax.ShapeDtypeStruct((t, d), F32),
        compiler_params=pltpu.CompilerParams(
            dimension_semantics=("parallel",), vmem_limit_bytes=VMEM_LIMIT),
        name="merge",
    )(x1, attn_o, conv_u, gates, wa, wc, wo)


def kernel(x, ffn1_norm_g, ffn1_w_gate, ffn1_w_up, ffn1_w_down, mix_norm_g, w_in, rel_bias_table, lambda_q1, lambda_k1, lambda_q2, lambda_k2, attn_head_norm_g, w_attn_branch, conv_dw_w, conv_dw_b, conv_ln_g, conv_ln_b, w_conv_branch, w_out, ffn2_norm_g, ffn2_w_gate, ffn2_w_up, ffn2_w_down, final_norm_g):
    batch, seq, d = x.shape
    depth = ffn1_w_gate.shape[0]
    attn_w = N_HEADS * HEAD_W
    conv_ch = conv_dw_w.shape[2]
    xt = x.reshape(batch * seq, d)
    row = lambda v: v.reshape(1, -1).astype(F32)

    for l in range(depth):
        lam_init = 0.8 - 0.6 * math.exp(-0.3 * l)
        last = l == depth - 1
        qk_cols = (0, 2 * attn_w)
        glu_cols = (3 * attn_w, 3 * attn_w + 2 * conv_ch)
        gate_cols = (glu_cols[1], w_in.shape[2])
        w_in_bf = w_in[l].astype(BF16)
        w_vt = w_in[l][:, 2 * attn_w:3 * attn_w].T.astype(BF16)

        x1, h2 = _ffn(xt, row(ffn1_norm_g[l]), ffn1_w_gate[l].astype(BF16),
                      ffn1_w_up[l].astype(BF16), ffn1_w_down[l].astype(BF16),
                      row(mix_norm_g[l]), final=False)

        qk = _mm(h2, w_in_bf, BF16, cols=qk_cols, tile_scales=(HEAD_DIM ** -0.5, LOG2E),
                 name="proj_qk")
        vt = _mm_nt(w_vt, h2, BF16, name="proj_vt")
        glu = _mm(h2, w_in_bf, F32, cols=glu_cols, name="proj_glu")
        conv_w = jnp.repeat(conv_dw_w[l].astype(F32), SUBLANES, axis=0)
        gates, conv_u = _gates_conv(h2, w_in_bf, glu, conv_w, row(conv_dw_b[l]),
                                    row(conv_ln_g[l]), row(conv_ln_b[l]), cols=gate_cols,
                                    seq=seq)

        later = (w_attn_branch[l], w_conv_branch[l], w_out[l],
                 ffn2_w_gate[l], ffn2_w_up[l], ffn2_w_down[l])
        lam_vecs = jnp.stack([lambda_q1[l], lambda_k1[l], lambda_q2[l], lambda_k2[l]]).astype(F32)
        attn_o, w_a, w_c, w_o, w2_gate, w2_up, w2_down = _attention(
            qk, vt, rel_bias_table.astype(F32), lam_vecs,
            attn_head_norm_g[l].reshape(HEAD_W, 1).astype(F32),
            batch=batch, seq=seq, lam_init=lam_init, casts=tuple(w.astype(F32) for w in later))
        x2 = _merge(x1, attn_o, conv_u, gates, w_a, w_c, w_o)

        xt = _ffn(x2, row(ffn2_norm_g[l]), w2_gate, w2_up, w2_down, row(final_norm_g),
                  final=last)[0]

    return xt.reshape(batch, seq, d)
```

```python
import functools
import math

import jax
import jax.numpy as jnp
from jax import lax
from jax.experimental import pallas as pl
from jax.experimental.pallas import tpu as pltpu

F32 = jnp.float32
BF16 = jnp.bfloat16

N_HEADS = 8
HEAD_DIM = 64
HEAD_W = 2 * HEAD_DIM
CONV_WIDTH = 31
NUM_BUCKETS = 32
MAX_EXACT = NUM_BUCKETS // 2
MAX_DISTANCE = 128
NEG_INF = -1e30
LOG2E = math.log2(math.e)
LANES = 128
BF16_ROWS = 16
SUBLANES = 8
MXU_N = 256
MXU_K = 256
MIB = 1024 * 1024
VMEM_LIMIT = 56 * MIB
FFN_VMEM_LIMIT = 60 * MIB

FFN_SPLIT = 2
ATT_BLK = 128
PV_KEYS = 256
ATT_GROUP = 4
CONV_HALO = 32
CONV_SHIFT_ROWS = 64


def _rms(x, g, eps):
    return x * lax.rsqrt(jnp.mean(x * x, axis=-1, keepdims=True) + eps) * g


def _sigmoid(x):
    return 1.0 / (1.0 + jnp.exp(-x))


def _ffn_body(x_ref, g_ref, wg_ref, wu_ref, wd_ref, ng_ref, *rest, final):
    if final:
        o_ref, h_ref = rest
    else:
        o_ref, hn_ref = rest
        h_ref = hn_ref
    f = pl.program_id(1)

    @pl.when(f == 0)
    def _():
        x = x_ref[...]
        h_ref[...] = _rms(x, g_ref[...], 1e-6).astype(BF16)
        o_ref[...] = x

    h = h_ref[...]
    tf = wg_ref.shape[1]
    halves = [slice(s * (tf // FFN_SPLIT), (s + 1) * (tf // FFN_SPLIT)) for s in range(FFN_SPLIT)]
    gu = [(jnp.dot(h, wg_ref[:, c], preferred_element_type=F32),
           jnp.dot(h, wu_ref[:, c], preferred_element_type=F32)) for c in halves]
    down = None
    for (gate, up), c in zip(gu, halves):
        act = ((gate * _sigmoid(gate)) * up * 0.5).astype(BF16)
        part = jnp.dot(act, wd_ref[c, :], preferred_element_type=F32)
        down = part if down is None else down + part
    o_ref[...] += down

    @pl.when(f == pl.num_programs(1) - 1)
    def _():
        y = _rms(o_ref[...], ng_ref[...], 1e-6)
        if final:
            o_ref[...] = y
        else:
            hn_ref[...] = y.astype(BF16)


def _ffn(x, g, wg, wu, wd, ng, *, final, bm=1024, tf=512):
    t, d = x.shape
    dff = wg.shape[1]
    grid = (t // bm, dff // tf)
    row = lambda i, f: (i, 0)
    out_shape = [jax.ShapeDtypeStruct((t, d), F32)]
    out_specs = [pl.BlockSpec((bm, d), row)]
    if not final:
        out_shape.append(jax.ShapeDtypeStruct((t, d), BF16))
        out_specs.append(pl.BlockSpec((bm, d), row))
    return pl.pallas_call(
        functools.partial(_ffn_body, final=final),
        grid=grid,
        in_specs=[
            pl.BlockSpec((bm, d), row),
            pl.BlockSpec((1, d), lambda i, f: (0, 0)),
            pl.BlockSpec((d, tf), lambda i, f: (0, f)),
            pl.BlockSpec((d, tf), lambda i, f: (0, f)),
            pl.BlockSpec((tf, d), lambda i, f: (f, 0)),
            pl.BlockSpec((1, d), lambda i, f: (0, 0)),
        ],
        out_specs=out_specs,
        out_shape=out_shape,
        scratch_shapes=[pltpu.VMEM((bm, d), BF16)] if final else [],
        compiler_params=pltpu.CompilerParams(
            dimension_semantics=("parallel", "arbitrary"), vmem_limit_bytes=FFN_VMEM_LIMIT),
        name="ffn_final" if final else "ffn",
    )(x, g, wg, wu, wd, ng)


def _mm_body(a_ref, b_ref, o_ref, *, tile_scales):
    acc = jnp.dot(a_ref[...], b_ref[...], preferred_element_type=F32)
    if tile_scales is not None:
        scale = jnp.float32(tile_scales[-1])
        for j, sc in enumerate(tile_scales[:-1]):
            scale = jnp.where(pl.program_id(1) == j, jnp.float32(sc), scale)
        acc = acc * scale
    o_ref[...] = acc.astype(o_ref.dtype)


def _mm(a, b, out_dtype, *, cols, bm=2048, tn=1024, tile_scales=None, name):
    m, k = a.shape
    n = cols[1] - cols[0]
    j0 = cols[0] // tn
    assert cols[0] % tn == 0 and n % tn == 0
    assert tile_scales is None or len(tile_scales) == n // tn
    return pl.pallas_call(
        functools.partial(_mm_body, tile_scales=tile_scales),
        grid=(m // bm, n // tn),
        in_specs=[pl.BlockSpec((bm, k), lambda i, j: (i, 0)),
                  pl.BlockSpec((k, tn), lambda i, j: (0, j0 + j))],
        out_specs=pl.BlockSpec((bm, tn), lambda i, j: (i, j)),
        out_shape=jax.ShapeDtypeStruct((m, n), out_dtype),
        compiler_params=pltpu.CompilerParams(
            dimension_semantics=("parallel", "arbitrary"), vmem_limit_bytes=VMEM_LIMIT),
        name=name,
    )(a, b)


def _mm_nt_body(wt_ref, a_ref, o_ref):
    o_ref[...] = lax.dot_general(wt_ref[...], a_ref[...], (((1,), (1,)), ((), ())),
                                 preferred_element_type=F32).astype(o_ref.dtype)


def _mm_nt(wt, a, out_dtype, *, bm=2048, name):
    n, k = wt.shape
    m = a.shape[0]
    return pl.pallas_call(
        _mm_nt_body,
        grid=(m // bm,),
        in_specs=[pl.BlockSpec((n, k), lambda i: (0, 0)),
                  pl.BlockSpec((bm, k), lambda i: (i, 0))],
        out_specs=pl.BlockSpec((n, bm), lambda i: (0, i)),
        out_shape=jax.ShapeDtypeStruct((n, m), out_dtype),
        compiler_params=pltpu.CompilerParams(
            dimension_semantics=("parallel",), vmem_limit_bytes=VMEM_LIMIT),
        name=name,
    )(wt, a)


def _t5_bias(d, tbl_ref, h):
    n = jnp.maximum(d, 0)
    nf = jnp.maximum(n, 1).astype(F32)
    large = MAX_EXACT + (jnp.log(nf / MAX_EXACT) / math.log(MAX_DISTANCE / MAX_EXACT)
                         * (NUM_BUCKETS - MAX_EXACT)).astype(jnp.int32)
    large = jnp.minimum(large, NUM_BUCKETS - 1)
    bucket = jnp.where(n < MAX_EXACT, n, large)
    b = jnp.zeros(d.shape, F32)
    for j in range(NUM_BUCKETS):
        b = jnp.where(bucket == j, tbl_ref[j, h], b)
    b = (b - tbl_ref[NUM_BUCKETS - 1, h]) * LOG2E
    return jnp.where(d >= 0, b, NEG_INF)


def _attn_body(tbl_ref, lam_ref, g_ref, q_ref, k_ref, vt_ref, *rest, lam_init, seq):
    n_cast = (len(rest) - 2) // 2
    cast_in, o_ref, cast_out, bias_ref = rest[:n_cast], rest[n_cast], rest[n_cast + 1:-1], rest[-1]
    for src, dst in zip(cast_in, cast_out):
        dst[...] = src[...].astype(BF16)

    h = pl.program_id(0)
    blk, grp = ATT_BLK, ATT_GROUP
    width = grp * blk
    nb = grp + 1

    @pl.when(pl.program_id(1) == 0)
    def _():
        kk = lax.broadcasted_iota(jnp.int32, (blk, blk), 0)
        jj = lax.broadcasted_iota(jnp.int32, (blk, blk), 1)
        diag = _t5_bias(jj - kk, tbl_ref, h)
        sub = _t5_bias(blk + jj - kk, tbl_ref, h)
        bias_ref[:(nb - 2) * blk, :] = jnp.zeros(((nb - 2) * blk, 2 * blk), F32)
        bias_ref[(nb - 2) * blk:(nb - 1) * blk, :] = jnp.concatenate([sub, sub], axis=1)
        bias_ref[(nb - 1) * blk:, :] = jnp.concatenate([diag, diag], axis=1)

    lv = lam_ref[...]
    lam = (jnp.exp(jnp.sum(lv[0:1] * lv[1:2], axis=-1, keepdims=True))
           - jnp.exp(jnp.sum(lv[2:3] * lv[3:4], axis=-1, keepdims=True)) + lam_init)
    g = g_ref[...]
    lane = lax.broadcasted_iota(jnp.int32, (blk, HEAD_W), 1)

    def load_qs(q0):
        out = []
        for r in range(grp):
            q = q_ref[pl.ds(q0 + r * blk, blk), :].astype(F32)
            out.append(jnp.concatenate([jnp.where(lane < HEAD_DIM, q, 0.0),
                                        jnp.where(lane >= HEAD_DIM, q, 0.0)],
                                       axis=0).astype(BF16))
        return out

    def scores(kb, qs, bias):
        st = lax.dot_general(kb, qs, (((1,), (1,)), ((), ())), preferred_element_type=F32)
        return st if bias is None else st + bias

    def update(state, st, vt):
        m, acc = state
        m_new = jnp.maximum(m, jnp.max(st, axis=0, keepdims=True))
        acc = jnp.exp2(m - m_new) * acc
        n = st.shape[0]
        for k0 in range(0, n, PV_KEYS):
            k1 = min(k0 + PV_KEYS, n)
            p = jnp.exp2(st[k0:k1] - m_new).astype(BF16)
            acc = acc + jnp.dot(vt[:, k0:k1], p, preferred_element_type=F32)
        return m_new, acc

    def finish(state, q0):
        _, acc = state
        o = acc[:HEAD_W] * (1.0 / acc[HEAD_W:HEAD_W + 1])
        ot = o[:, :blk] - lam * o[:, blk:]
        y = ot * lax.rsqrt(jnp.mean(ot * ot, axis=0, keepdims=True) + 1e-5) * g
        y = y * (1.0 - lam_init)
        o_ref[pl.ds(q0, blk), :] = y.T.astype(o_ref.dtype)

    init = ((jnp.full((1, 2 * blk), NEG_INF, F32),
             jnp.zeros((HEAD_W + BF16_ROWS, 2 * blk), F32)),) * grp
    ones_rows = jnp.ones((BF16_ROWS, width), BF16)

    steps = []
    for gi in range(seq // width):
        kinds = ["far"] * max(gi - 1, 0) + (["prev"] if gi else []) + ["own"]
        for n, kind in enumerate(kinds):
            steps.append((gi, kind, n * width, n == len(kinds) - 1))

    def chain_scores(step, qs, r):
        gi, kind, k0, _ = step
        if kind == "far":
            return scores(k_ref[pl.ds(k0, width), :], qs[r], None)
        if kind == "prev":
            return scores(k_ref[pl.ds(k0, width), :], qs[r], bias_ref[:width, :] if r == 0 else None)
        return scores(k_ref[pl.ds(k0, (r + 1) * blk), :], qs[r], bias_ref[(grp - r) * blk:, :])

    qs = load_qs(0)
    cur = [chain_scores(steps[0], qs, r) for r in range(grp)]
    states = list(init)
    for n, step in enumerate(steps):
        gi, _, k0, last = step
        more = n + 1 < len(steps)
        if more and last:
            qs = load_qs(steps[n + 1][0] * width)
        vt = jnp.concatenate([vt_ref[:, pl.ds(k0, width)], ones_rows], axis=0)
        nxt = []
        for r in range(grp):
            states[r] = update(states[r], cur[r], vt[:, :cur[r].shape[0]])
            if more:
                nxt.append(chain_scores(steps[n + 1], qs, r))
        if last:
            for r in range(grp):
                finish(states[r], gi * width + r * blk)
            states = list(init)
        cur = nxt


def _attention(qk, vt, tbl, lam_vecs, g_col, *, batch, seq, lam_init, casts=()):
    t = qk.shape[0]
    n_steps = batch * N_HEADS
    slab = lambda h, b: (h * batch + b, 0)
    for w in casts:
        assert w.shape[0] % (n_steps * BF16_ROWS) == 0, w.shape
    cast_specs = [pl.BlockSpec((w.shape[0] // n_steps, w.shape[1]), slab) for w in casts]
    return pl.pallas_call(
        functools.partial(_attn_body, lam_init=lam_init, seq=seq),
        grid=(N_HEADS, batch),
        in_specs=[
            pl.BlockSpec(memory_space=pltpu.SMEM),
            pl.BlockSpec((4, HEAD_DIM), lambda h, b: (0, 0)),
            pl.BlockSpec((HEAD_W, 1), lambda h, b: (0, 0)),
            pl.BlockSpec((seq, HEAD_W), lambda h, b: (b, h)),
            pl.BlockSpec((seq, HEAD_W), lambda h, b: (b, N_HEADS + h)),
            pl.BlockSpec((HEAD_W, seq), lambda h, b: (h, b)),
        ] + cast_specs,
        out_specs=[pl.BlockSpec((seq, HEAD_W), lambda h, b: (b, h))] + cast_specs,
        out_shape=[jax.ShapeDtypeStruct((t, N_HEADS * HEAD_W), BF16)]
        + [jax.ShapeDtypeStruct(w.shape, BF16) for w in casts],
        scratch_shapes=[pltpu.VMEM(((ATT_GROUP + 1) * ATT_BLK, 2 * ATT_BLK), F32)],
        compiler_params=pltpu.CompilerParams(
            dimension_semantics=("parallel", "arbitrary"), vmem_limit_bytes=VMEM_LIMIT),
        name="diff_attn",
    )(tbl, lam_vecs, g_col, qk, qk, vt, *casts)


def _conv_prepare(first, cur_ref, halo_ref, ext_ref, bs, c):
    cur = cur_ref[...]
    ext_ref[0, CONV_HALO:, :] = cur[:, :c] * _sigmoid(cur[:, c:])
    hal = halo_ref[...]
    uh = hal[:, :c] * _sigmoid(hal[:, c:])
    ext_ref[0, :CONV_HALO, :] = jnp.where(first, 0.0, uh)

    def shift_rows(r0, n):
        win = ext_ref[0, r0:r0 + n + SUBLANES, :]
        for s in range(1, SUBLANES):
            ext_ref[s, r0:r0 + n, :] = win[s:s + n]

    for r0 in range(0, bs, CONV_SHIFT_ROWS):
        shift_rows(r0, CONV_SHIFT_ROWS)
    shift_rows(bs, CONV_HALO - SUBLANES)


def _conv_group(r0, w_ref, b_ref, g_ref, beta_ref, o_ref, ext_ref):
    acc = jnp.zeros((SUBLANES, o_ref.shape[1]), F32)
    for j in range(CONV_WIDTH):
        off = CONV_HALO - (CONV_WIDTH - 1) + j
        s, a = off % SUBLANES, off // SUBLANES
        start = r0 + a * SUBLANES
        acc = acc + w_ref[j * SUBLANES:(j + 1) * SUBLANES, :] * ext_ref[s, start:start + SUBLANES, :]
    y = acc + b_ref[...]
    mu = jnp.mean(y, axis=-1, keepdims=True)
    var = jnp.mean(jnp.square(y - mu), axis=-1, keepdims=True)
    yn = (y - mu) * lax.rsqrt(var + 1e-5) * g_ref[...] + beta_ref[...]
    out = yn * _sigmoid(yn)
    o_ref[r0:r0 + SUBLANES, :] = out.astype(o_ref.dtype)
    return out


def _zero_after(x):
    bits = pltpu.bitcast(jnp.concatenate([x[:SUBLANES, :LANES]] * 2, axis=0), jnp.uint32)
    bits = lax.shift_right_logical(lax.shift_right_logical(bits, jnp.uint32(16)), jnp.uint32(16))
    return pltpu.bitcast(bits, F32).astype(BF16)


def _gates_conv_body(h_ref, wg_ref, cur_ref, halo_ref, w_ref, b_ref, g_ref, beta_ref,
                     gate_o_ref, conv_o_ref, ext_ref, *, blocks_per_seq):
    bs, c = conv_o_ref.shape
    tn = wg_ref.shape[1]
    blk = pl.program_id(0) * pl.num_programs(1) + pl.program_id(1)
    _conv_prepare(blk % blocks_per_seq == 0, cur_ref, halo_ref, ext_ref, bs, c)
    n_chunks = tn // MXU_N
    k_pieces = h_ref.shape[1] // MXU_K
    per_piece = (bs // SUBLANES) // (n_chunks * k_pieces)
    anchor = None
    r0 = 0
    for ci in range(n_chunks):
        cols = slice(ci * MXU_N, (ci + 1) * MXU_N)
        acc = None
        for ki in range(k_pieces):
            rows = slice(ki * MXU_K, (ki + 1) * MXU_K)
            w = wg_ref[rows, cols]
            if anchor is not None:
                top = jnp.concatenate([w[:BF16_ROWS, :LANES] + anchor, w[:BF16_ROWS, LANES:]],
                                      axis=1)
                w = jnp.concatenate(
                    [top, wg_ref[ki * MXU_K + BF16_ROWS:(ki + 1) * MXU_K, cols]], axis=0)
            part = jnp.dot(h_ref[:, rows], w, preferred_element_type=F32)
            acc = part if acc is None else acc + part
            for _ in range(per_piece):
                out = _conv_group(r0, w_ref, b_ref, g_ref, beta_ref, conv_o_ref, ext_ref)
                r0 += SUBLANES
            anchor = _zero_after(out)
        gate_o_ref[:, cols] = acc
    assert r0 == bs


def _gates_conv(h2, w_in, glu, w, b, g, beta, *, cols, seq, bm=1024, tn=1024):
    t, k = h2.shape
    n = cols[1] - cols[0]
    j0 = cols[0] // tn
    assert cols[0] % tn == 0 and n % tn == 0
    c2 = glu.shape[1]
    c = c2 // 2
    steps = n // tn
    bs = bm // steps
    per = bs // CONV_HALO
    vec = lambda i, j: (0, 0)
    return pl.pallas_call(
        functools.partial(_gates_conv_body, blocks_per_seq=seq // bs),
        grid=(t // bm, steps),
        in_specs=[
            pl.BlockSpec((bm, k), lambda i, j: (i, 0)),
            pl.BlockSpec((k, tn), lambda i, j: (0, j0 + j)),
            pl.BlockSpec((bs, c2), lambda i, j: (i * steps + j, 0)),
            pl.BlockSpec((CONV_HALO, c2),
                         lambda i, j: (jnp.maximum((i * steps + j) * per - 1, 0), 0)),
            pl.BlockSpec((CONV_WIDTH * SUBLANES, c), vec),
            pl.BlockSpec((1, c), vec),
            pl.BlockSpec((1, c), vec),
            pl.BlockSpec((1, c), vec),
        ],
        out_specs=[pl.BlockSpec((bm, tn), lambda i, j: (i, j)),
                   pl.BlockSpec((bs, c), lambda i, j: (i * steps + j, 0))],
        out_shape=[jax.ShapeDtypeStruct((t, n), F32), jax.ShapeDtypeStruct((t, c), BF16)],
        scratch_shapes=[pltpu.VMEM((SUBLANES, CONV_HALO + bs, c), F32)],
        compiler_params=pltpu.CompilerParams(
            dimension_semantics=("parallel", "arbitrary"), vmem_limit_bytes=VMEM_LIMIT),
        name="gates_conv",
    )(h2, w_in, glu, glu, w, b, g, beta)


def _merge_body(x_ref, a_ref, u_ref, gate_ref, wa_ref, wc_ref, wo_ref, o_ref):
    d = o_ref.shape[1]
    ab = jnp.dot(a_ref[...], wa_ref[...], preferred_element_type=F32)
    cb = jnp.dot(u_ref[...], wc_ref[...], preferred_element_type=F32)
    gates = gate_ref[...]
    merged = _sigmoid(gates[:, :d]) * ab + _sigmoid(gates[:, d:]) * cb
    o_ref[...] = x_ref[...] + jnp.dot(merged.astype(BF16), wo_ref[...],
                                      preferred_element_type=F32)


def _merge(x1, attn_o, conv_u, gates, wa, wc, wo, *, bm=512):
    t, d = x1.shape
    row = lambda i: (i, 0)
    const = lambda i: (0, 0)
    resident = dict(pipeline_mode=pl.Buffered(1))
    return pl.pallas_call(
        _merge_body,
        grid=(t // bm,),
        in_specs=[
            pl.BlockSpec((bm, d), row),
            pl.BlockSpec((bm, attn_o.shape[1]), row),
            pl.BlockSpec((bm, conv_u.shape[1]), row),
            pl.BlockSpec((bm, 2 * d), row),
            pl.BlockSpec(wa.shape, const, **resident),
            pl.BlockSpec(wc.shape, const, **resident),
            pl.BlockSpec(wo.shape, const, **resident),
        ],
        out_specs=pl.BlockSpec((bm, d), row),
        out_shape=jax.ShapeDtypeStruct((t, d), F32),
        compiler_params=pltpu.CompilerParams(
            dimension_semantics=("parallel",), vmem_limit_bytes=FFN_VMEM_LIMIT),
        name="merge",
    )(x1, attn_o, conv_u, gates, wa, wc, wo)


def kernel(x, ffn1_norm_g, ffn1_w_gate, ffn1_w_up, ffn1_w_down, mix_norm_g, w_in, rel_bias_table, lambda_q1, lambda_k1, lambda_q2, lambda_k2, attn_head_norm_g, w_attn_branch, conv_dw_w, conv_dw_b, conv_ln_g, conv_ln_b, w_conv_branch, w_out, ffn2_norm_g, ffn2_w_gate, ffn2_w_up, ffn2_w_down, final_norm_g):
    batch, seq, d = x.shape
    depth = ffn1_w_gate.shape[0]
    attn_w = N_HEADS * HEAD_W
    conv_ch = conv_dw_w.shape[2]
    xt = x.reshape(batch * seq, d)
    row = lambda v: v.reshape(1, -1).astype(F32)

    for l in range(depth):
        lam_init = 0.8 - 0.6 * math.exp(-0.3 * l)
        last = l == depth - 1
        qk_cols = (0, 2 * attn_w)
        glu_cols = (3 * attn_w, 3 * attn_w + 2 * conv_ch)
        gate_cols = (glu_cols[1], w_in.shape[2])
        w_in_bf = w_in[l].astype(BF16)
        w_vt = w_in[l][:, 2 * attn_w:3 * attn_w].T.astype(BF16)

        x1, h2 = _ffn(xt, row(ffn1_norm_g[l]), ffn1_w_gate[l].astype(BF16),
                      ffn1_w_up[l].astype(BF16), ffn1_w_down[l].astype(BF16),
                      row(mix_norm_g[l]), final=False)

        qk = _mm(h2, w_in_bf, BF16, cols=qk_cols, tile_scales=(HEAD_DIM ** -0.5, LOG2E),
                 name="proj_qk")
        vt = _mm_nt(w_vt, h2, BF16, name="proj_vt")
        glu = _mm(h2, w_in_bf, F32, cols=glu_cols, name="proj_glu")
        conv_w = jnp.repeat(conv_dw_w[l].astype(F32), SUBLANES, axis=0)
        gates, conv_u = _gates_conv(h2, w_in_bf, glu, conv_w, row(conv_dw_b[l]),
                                    row(conv_ln_g[l]), row(conv_ln_b[l]), cols=gate_cols,
                                    seq=seq)

        later = (w_attn_branch[l], w_conv_branch[l], w_out[l],
                 ffn2_w_gate[l], ffn2_w_up[l], ffn2_w_down[l])
        lam_vecs = jnp.stack([lambda_q1[l], lambda_k1[l], lambda_q2[l], lambda_k2[l]]).astype(F32)
        attn_o, w_a, w_c, w_o, w2_gate, w2_up, w2_down = _attention(
            qk, vt, rel_bias_table.astype(F32), lam_vecs,
            attn_head_norm_g[l].reshape(HEAD_W, 1).astype(F32),
            batch=batch, seq=seq, lam_init=lam_init, casts=tuple(w.astype(F32) for w in later))
        x2 = _merge(x1, attn_o, conv_u, gates, w_a, w_c, w_o)

        xt = _ffn(x2, row(ffn2_norm_g[l]), w2_gate, w2_up, w2_down, row(final_norm_g),
                  final=last)[0]

    return xt.reshape(batch, seq, d)
```

```python
import functools
import math

import jax
import jax.numpy as jnp
from jax import lax
from jax.experimental import pallas as pl
from jax.experimental.pallas import tpu as pltpu

F32 = jnp.float32
BF16 = jnp.bfloat16

N_HEADS = 8
HEAD_DIM = 64
HEAD_W = 2 * HEAD_DIM
CONV_WIDTH = 31
NUM_BUCKETS = 32
MAX_EXACT = NUM_BUCKETS // 2
MAX_DISTANCE = 128
NEG_INF = -1e30
LOG2E = math.log2(math.e)
LANES = 128
BF16_ROWS = 16
SUBLANES = 8
MXU_N = 256
MXU_K = 256
MIB = 1024 * 1024
VMEM_LIMIT = 56 * MIB
FFN_VMEM_LIMIT = 60 * MIB

FFN_SPLIT = 2
ATT_BLK = 128
PV_KEYS = 256
ATT_GROUP = 4
CONV_HALO = 32
CONV_SHIFT_ROWS = 64


def _rms(x, g, eps):
    return x * lax.rsqrt(jnp.mean(x * x, axis=-1, keepdims=True) + eps) * g


def _sigmoid(x):
    return 1.0 / (1.0 + jnp.exp(-x))


def _ffn_body(x_ref, g_ref, wg_ref, wu_ref, wd_ref, ng_ref, *rest, final):
    if final:
        o_ref, h_ref = rest
    else:
        o_ref, hn_ref = rest
        h_ref = hn_ref
    f = pl.program_id(1)

    @pl.when(f == 0)
    def _():
        x = x_ref[...]
        h_ref[...] = _rms(x, g_ref[...], 1e-6).astype(BF16)
        o_ref[...] = x

    h = h_ref[...]
    tf = wg_ref.shape[1]
    halves = [slice(s * (tf // FFN_SPLIT), (s + 1) * (tf // FFN_SPLIT)) for s in range(FFN_SPLIT)]
    gu = [(jnp.dot(h, wg_ref[:, c], preferred_element_type=F32),
           jnp.dot(h, wu_ref[:, c], preferred_element_type=F32)) for c in halves]
    down = None
    for (gate, up), c in zip(gu, halves):
        act = ((0.25 * gate) * (jnp.tanh(0.5 * gate) + 1.0) * up).astype(BF16)
        part = jnp.dot(act, wd_ref[c, :], preferred_element_type=F32)
        down = part if down is None else down + part
    o_ref[...] += down

    @pl.when(f == pl.num_programs(1) - 1)
    def _():
        y = _rms(o_ref[...], ng_ref[...], 1e-6)
        if final:
            o_ref[...] = y
        else:
            hn_ref[...] = y.astype(BF16)


def _ffn(x, g, wg, wu, wd, ng, *, final, bm=1024, tf=512):
    t, d = x.shape
    dff = wg.shape[1]
    grid = (t // bm, dff // tf)
    row = lambda i, f: (i, 0)
    out_shape = [jax.ShapeDtypeStruct((t, d), F32)]
    out_specs = [pl.BlockSpec((bm, d), row)]
    if not final:
        out_shape.append(jax.ShapeDtypeStruct((t, d), BF16))
        out_specs.append(pl.BlockSpec((bm, d), row))
    return pl.pallas_call(
        functools.partial(_ffn_body, final=final),
        grid=grid,
        in_specs=[
            pl.BlockSpec((bm, d), row),
            pl.BlockSpec((1, d), lambda i, f: (0, 0)),
            pl.BlockSpec((d, tf), lambda i, f: (0, f)),
            pl.BlockSpec((d, tf), lambda i, f: (0, f)),
            pl.BlockSpec((tf, d), lambda i, f: (f, 0)),
            pl.BlockSpec((1, d), lambda i, f: (0, 0)),
        ],
        out_specs=out_specs,
        out_shape=out_shape,
        scratch_shapes=[pltpu.VMEM((bm, d), BF16)] if final else [],
        compiler_params=pltpu.CompilerParams(
            dimension_semantics=("parallel", "arbitrary"), vmem_limit_bytes=FFN_VMEM_LIMIT),
        name="ffn_final" if final else "ffn",
    )(x, g, wg, wu, wd, ng)


def _mm_body(a_ref, b_ref, o_ref, *, tile_scales):
    acc = jnp.dot(a_ref[...], b_ref[...], preferred_element_type=F32)
    if tile_scales is not None:
        scale = jnp.float32(tile_scales[-1])
        for j, sc in enumerate(tile_scales[:-1]):
            scale = jnp.where(pl.program_id(1) == j, jnp.float32(sc), scale)
        acc = acc * scale
    o_ref[...] = acc.astype(o_ref.dtype)


def _mm(a, b, out_dtype, *, cols, bm=2048, tn=1024, tile_scales=None, name):
    m, k = a.shape
    n = cols[1] - cols[0]
    j0 = cols[0] // tn
    assert cols[0] % tn == 0 and n % tn == 0
    assert tile_scales is None or len(tile_scales) == n // tn
    return pl.pallas_call(
        functools.partial(_mm_body, tile_scales=tile_scales),
        grid=(m // bm, n // tn),
        in_specs=[pl.BlockSpec((bm, k), lambda i, j: (i, 0)),
                  pl.BlockSpec((k, tn), lambda i, j: (0, j0 + j))],
        out_specs=pl.BlockSpec((bm, tn), lambda i, j: (i, j)),
        out_shape=jax.ShapeDtypeStruct((m, n), out_dtype),
        compiler_params=pltpu.CompilerParams(
            dimension_semantics=("parallel", "arbitrary"), vmem_limit_bytes=VMEM_LIMIT),
        name=name,
    )(a, b)


def _mm_nt_body(wt_ref, a_ref, o_ref):
    o_ref[...] = lax.dot_general(wt_ref[...], a_ref[...], (((1,), (1,)), ((), ())),
                                 preferred_element_type=F32).astype(o_ref.dtype)


def _mm_nt(wt, a, out_dtype, *, bm=2048, name):
    n, k = wt.shape
    m = a.shape[0]
    return pl.pallas_call(
        _mm_nt_body,
        grid=(m // bm,),
        in_specs=[pl.BlockSpec((n, k), lambda i: (0, 0)),
                  pl.BlockSpec((bm, k), lambda i: (i, 0))],
        out_specs=pl.BlockSpec((n, bm), lambda i: (0, i)),
        out_shape=jax.ShapeDtypeStruct((n, m), out_dtype),
        compiler_params=pltpu.CompilerParams(
            dimension_semantics=("parallel",), vmem_limit_bytes=VMEM_LIMIT),
        name=name,
    )(wt, a)


def _t5_bias(d, tbl_ref, h):
    n = jnp.maximum(d, 0)
    nf = jnp.maximum(n, 1).astype(F32)
    large = MAX_EXACT + (jnp.log(nf / MAX_EXACT) / math.log(MAX_DISTANCE / MAX_EXACT)
                         * (NUM_BUCKETS - MAX_EXACT)).astype(jnp.int32)
    large = jnp.minimum(large, NUM_BUCKETS - 1)
    bucket = jnp.where(n < MAX_EXACT, n, large)
    b = jnp.zeros(d.shape, F32)
    for j in range(NUM_BUCKETS):
        b = jnp.where(bucket == j, tbl_ref[j, h], b)
    b = (b - tbl_ref[NUM_BUCKETS - 1, h]) * LOG2E
    return jnp.where(d >= 0, b, NEG_INF)


def _attn_body(tbl_ref, lam_ref, g_ref, q_ref, k_ref, vt_ref, *rest, lam_init, seq):
    n_cast = (len(rest) - 2) // 2
    cast_in, o_ref, cast_out, bias_ref = rest[:n_cast], rest[n_cast], rest[n_cast + 1:-1], rest[-1]
    for src, dst in zip(cast_in, cast_out):
        dst[...] = src[...].astype(BF16)

    h = pl.program_id(0)
    blk, grp = ATT_BLK, ATT_GROUP
    width = grp * blk
    nb = grp + 1

    @pl.when(pl.program_id(1) == 0)
    def _():
        kk = lax.broadcasted_iota(jnp.int32, (blk, blk), 0)
        jj = lax.broadcasted_iota(jnp.int32, (blk, blk), 1)
        diag = _t5_bias(jj - kk, tbl_ref, h)
        sub = _t5_bias(blk + jj - kk, tbl_ref, h)
        bias_ref[:(nb - 2) * blk, :] = jnp.zeros(((nb - 2) * blk, 2 * blk), F32)
        bias_ref[(nb - 2) * blk:(nb - 1) * blk, :] = jnp.concatenate([sub, sub], axis=1)
        bias_ref[(nb - 1) * blk:, :] = jnp.concatenate([diag, diag], axis=1)

    lv = lam_ref[...]
    lam = (jnp.exp(jnp.sum(lv[0:1] * lv[1:2], axis=-1, keepdims=True))
           - jnp.exp(jnp.sum(lv[2:3] * lv[3:4], axis=-1, keepdims=True)) + lam_init)
    g = g_ref[...]
    lane = lax.broadcasted_iota(jnp.int32, (blk, HEAD_W), 1)

    def load_qs(q0):
        out = []
        for r in range(grp):
            q = q_ref[pl.ds(q0 + r * blk, blk), :].astype(F32)
            out.append(jnp.concatenate([jnp.where(lane < HEAD_DIM, q, 0.0),
                                        jnp.where(lane >= HEAD_DIM, q, 0.0)],
                                       axis=0).astype(BF16))
        return out

    def scores(kb, qs, bias):
        st = lax.dot_general(kb, qs, (((1,), (1,)), ((), ())), preferred_element_type=F32)
        return st if bias is None else st + bias

    def update(state, st, vt):
        m, acc = state
        m_new = jnp.maximum(m, jnp.max(st, axis=0, keepdims=True))
        acc = jnp.exp2(m - m_new) * acc
        n = st.shape[0]
        for k0 in range(0, n, PV_KEYS):
            k1 = min(k0 + PV_KEYS, n)
            p = jnp.exp2(st[k0:k1] - m_new).astype(BF16)
            acc = acc + jnp.dot(vt[:, k0:k1], p, preferred_element_type=F32)
        return m_new, acc

    def finish(state, q0):
        _, acc = state
        o = acc[:HEAD_W] * (1.0 / acc[HEAD_W:HEAD_W + 1])
        ot = o[:, :blk] - lam * o[:, blk:]
        y = ot * lax.rsqrt(jnp.mean(ot * ot, axis=0, keepdims=True) + 1e-5) * g
        y = y * (1.0 - lam_init)
        o_ref[pl.ds(q0, blk), :] = y.T.astype(o_ref.dtype)

    init = ((jnp.full((1, 2 * blk), NEG_INF, F32),
             jnp.zeros((HEAD_W + BF16_ROWS, 2 * blk), F32)),) * grp
    ones_rows = jnp.ones((BF16_ROWS, width), BF16)

    steps = []
    for gi in range(seq // width):
        kinds = ["far"] * max(gi - 1, 0) + (["prev"] if gi else []) + ["own"]
        for n, kind in enumerate(kinds):
            steps.append((gi, kind, n * width, n == len(kinds) - 1))

    def chain_scores(step, qs, r):
        gi, kind, k0, _ = step
        if kind == "far":
            return scores(k_ref[pl.ds(k0, width), :], qs[r], None)
        if kind == "prev":
            return scores(k_ref[pl.ds(k0, width), :], qs[r], bias_ref[:width, :] if r == 0 else None)
        return scores(k_ref[pl.ds(k0, (r + 1) * blk), :], qs[r], bias_ref[(grp - r) * blk:, :])

    qs = load_qs(0)
    cur = [chain_scores(steps[0], qs, r) for r in range(grp)]
    states = list(init)
    for n, step in enumerate(steps):
        gi, _, k0, last = step
        more = n + 1 < len(steps)
        if more and last:
            qs = load_qs(steps[n + 1][0] * width)
        vt = jnp.concatenate([vt_ref[:, pl.ds(k0, width)], ones_rows], axis=0)
        nxt = []
        for r in range(grp):
            states[r] = update(states[r], cur[r], vt[:, :cur[r].shape[0]])
            if more:
                nxt.append(chain_scores(steps[n + 1], qs, r))
        if last:
            for r in range(grp):
                finish(states[r], gi * width + r * blk)
            states = list(init)
        cur = nxt


def _attention(qk, vt, tbl, lam_vecs, g_col, *, batch, seq, lam_init, casts=()):
    t = qk.shape[0]
    n_steps = batch * N_HEADS
    slab = lambda h, b: (h * batch + b, 0)
    for w in casts:
        assert w.shape[0] % (n_steps * BF16_ROWS) == 0, w.shape
    cast_specs = [pl.BlockSpec((w.shape[0] // n_steps, w.shape[1]), slab) for w in casts]
    return pl.pallas_call(
        functools.partial(_attn_body, lam_init=lam_init, seq=seq),
        grid=(N_HEADS, batch),
        in_specs=[
            pl.BlockSpec(memory_space=pltpu.SMEM),
            pl.BlockSpec((4, HEAD_DIM), lambda h, b: (0, 0)),
            pl.BlockSpec((HEAD_W, 1), lambda h, b: (0, 0)),
            pl.BlockSpec((seq, HEAD_W), lambda h, b: (b, h)),
            pl.BlockSpec((seq, HEAD_W), lambda h, b: (b, N_HEADS + h)),
            pl.BlockSpec((HEAD_W, seq), lambda h, b: (h, b)),
        ] + cast_specs,
        out_specs=[pl.BlockSpec((seq, HEAD_W), lambda h, b: (b, h))] + cast_specs,
        out_shape=[jax.ShapeDtypeStruct((t, N_HEADS * HEAD_W), BF16)]
        + [jax.ShapeDtypeStruct(w.shape, BF16) for w in casts],
        scratch_shapes=[pltpu.VMEM(((ATT_GROUP + 1) * ATT_BLK, 2 * ATT_BLK), F32)],
        compiler_params=pltpu.CompilerParams(
            dimension_semantics=("parallel", "arbitrary"), vmem_limit_bytes=VMEM_LIMIT),
        name="diff_attn",
    )(tbl, lam_vecs, g_col, qk, qk, vt, *casts)


def _conv_prepare(first, cur_ref, halo_ref, ext_ref, bs, c):
    cur = cur_ref[...]
    ext_ref[0, CONV_HALO:, :] = cur[:, :c] * _sigmoid(cur[:, c:])
    hal = halo_ref[...]
    uh = hal[:, :c] * _sigmoid(hal[:, c:])
    ext_ref[0, :CONV_HALO, :] = jnp.where(first, 0.0, uh)

    def shift_rows(r0, n):
        win = ext_ref[0, r0:r0 + n + SUBLANES, :]
        for s in range(1, SUBLANES):
            ext_ref[s, r0:r0 + n, :] = win[s:s + n]

    for r0 in range(0, bs, CONV_SHIFT_ROWS):
        shift_rows(r0, CONV_SHIFT_ROWS)
    shift_rows(bs, CONV_HALO - SUBLANES)


def _conv_group(r0, w_ref, b_ref, g_ref, beta_ref, o_ref, ext_ref):
    acc = jnp.zeros((SUBLANES, o_ref.shape[1]), F32)
    for j in range(CONV_WIDTH):
        off = CONV_HALO - (CONV_WIDTH - 1) + j
        s, a = off % SUBLANES, off // SUBLANES
        start = r0 + a * SUBLANES
        acc = acc + w_ref[j * SUBLANES:(j + 1) * SUBLANES, :] * ext_ref[s, start:start + SUBLANES, :]
    y = acc + b_ref[...]
    mu = jnp.mean(y, axis=-1, keepdims=True)
    var = jnp.mean(jnp.square(y - mu), axis=-1, keepdims=True)
    yn = (y - mu) * lax.rsqrt(var + 1e-5) * g_ref[...] + beta_ref[...]
    out = yn * _sigmoid(yn)
    o_ref[r0:r0 + SUBLANES, :] = out.astype(o_ref.dtype)
    return out


def _zero_after(x):
    bits = pltpu.bitcast(jnp.concatenate([x[:SUBLANES, :LANES]] * 2, axis=0), jnp.uint32)
    bits = lax.shift_right_logical(lax.shift_right_logical(bits, jnp.uint32(16)), jnp.uint32(16))
    return pltpu.bitcast(bits, F32).astype(BF16)


def _gates_conv_body(h_ref, wg_ref, cur_ref, halo_ref, w_ref, b_ref, g_ref, beta_ref,
                     gate_o_ref, conv_o_ref, ext_ref, *, blocks_per_seq):
    bs, c = conv_o_ref.shape
    tn = wg_ref.shape[1]
    blk = pl.program_id(0) * pl.num_programs(1) + pl.program_id(1)
    _conv_prepare(blk % blocks_per_seq == 0, cur_ref, halo_ref, ext_ref, bs, c)
    n_chunks = tn // MXU_N
    k_pieces = h_ref.shape[1] // MXU_K
    per_piece = (bs // SUBLANES) // (n_chunks * k_pieces)
    anchor = None
    r0 = 0
    for ci in range(n_chunks):
        cols = slice(ci * MXU_N, (ci + 1) * MXU_N)
        acc = None
        for ki in range(k_pieces):
            rows = slice(ki * MXU_K, (ki + 1) * MXU_K)
            w = wg_ref[rows, cols]
            if anchor is not None:
                top = jnp.concatenate([w[:BF16_ROWS, :LANES] + anchor, w[:BF16_ROWS, LANES:]],
                                      axis=1)
                w = jnp.concatenate(
                    [top, wg_ref[ki * MXU_K + BF16_ROWS:(ki + 1) * MXU_K, cols]], axis=0)
            part = jnp.dot(h_ref[:, rows], w, preferred_element_type=F32)
            acc = part if acc is None else acc + part
            for _ in range(per_piece):
                out = _conv_group(r0, w_ref, b_ref, g_ref, beta_ref, conv_o_ref, ext_ref)
                r0 += SUBLANES
            anchor = _zero_after(out)
        gate_o_ref[:, cols] = acc
    assert r0 == bs


def _gates_conv(h2, w_in, glu, w, b, g, beta, *, cols, seq, bm=1024, tn=1024):
    t, k = h2.shape
    n = cols[1] - cols[0]
    j0 = cols[0] // tn
    assert cols[0] % tn == 0 and n % tn == 0
    c2 = glu.shape[1]
    c = c2 // 2
    steps = n // tn
    bs = bm // steps
    per = bs // CONV_HALO
    vec = lambda i, j: (0, 0)
    return pl.pallas_call(
        functools.partial(_gates_conv_body, blocks_per_seq=seq // bs),
        grid=(t // bm, steps),
        in_specs=[
            pl.BlockSpec((bm, k), lambda i, j: (i, 0)),
            pl.BlockSpec((k, tn), lambda i, j: (0, j0 + j)),
            pl.BlockSpec((bs, c2), lambda i, j: (i * steps + j, 0)),
            pl.BlockSpec((CONV_HALO, c2),
                         lambda i, j: (jnp.maximum((i * steps + j) * per - 1, 0), 0)),
            pl.BlockSpec((CONV_WIDTH * SUBLANES, c), vec),
            pl.BlockSpec((1, c), vec),
            pl.BlockSpec((1, c), vec),
            pl.BlockSpec((1, c), vec),
        ],
        out_specs=[pl.BlockSpec((bm, tn), lambda i, j: (i, j)),
                   pl.BlockSpec((bs, c), lambda i, j: (i * steps + j, 0))],
        out_shape=[jax.ShapeDtypeStruct((t, n), F32), jax.ShapeDtypeStruct((t, c), BF16)],
        scratch_shapes=[pltpu.VMEM((SUBLANES, CONV_HALO + bs, c), F32)],
        compiler_params=pltpu.CompilerParams(
            dimension_semantics=("parallel", "arbitrary"), vmem_limit_bytes=VMEM_LIMIT),
        name="gates_conv",
    )(h2, w_in, glu, glu, w, b, g, beta)


def _merge_body(x_ref, a_ref, u_ref, gate_ref, wa_ref, wc_ref, wo_ref, o_ref):
    d = o_ref.shape[1]
    ab = jnp.dot(a_ref[...], wa_ref[...], preferred_element_type=F32)
    cb = jnp.dot(u_ref[...], wc_ref[...], preferred_element_type=F32)
    gates = gate_ref[...]
    merged = _sigmoid(gates[:, :d]) * ab + _sigmoid(gates[:, d:]) * cb
    o_ref[...] = x_ref[...] + jnp.dot(merged.astype(BF16), wo_ref[...],
                                      preferred_element_type=F32)


def _merge(x1, attn_o, conv_u, gates, wa, wc, wo, *, bm=256):
    t, d = x1.shape
    row = lambda i: (i, 0)
    const = lambda i: (0, 0)
    resident = dict(pipeline_mode=pl.Buffered(1))
    return pl.pallas_call(
        _merge_body,
        grid=(t // bm,),
        in_specs=[
            pl.BlockSpec((bm, d), row),
            pl.BlockSpec((bm, attn_o.shape[1]), row),
            pl.BlockSpec((bm, conv_u.shape[1]), row),
            pl.BlockSpec((bm, 2 * d), row),
            pl.BlockSpec(wa.shape, const, **resident),
            pl.BlockSpec(wc.shape, const, **resident),
            pl.BlockSpec(wo.shape, const, **resident),
        ],
        out_specs=pl.BlockSpec((bm, d), row),
        out_shape=jax.ShapeDtypeStruct((t, d), F32),
        compiler_params=pltpu.CompilerParams(
            dimension_semantics=("parallel",), vmem_limit_bytes=VMEM_LIMIT),
        name="merge",
    )(x1, attn_o, conv_u, gates, wa, wc, wo)


def kernel(x, ffn1_norm_g, ffn1_w_gate, ffn1_w_up, ffn1_w_down, mix_norm_g, w_in, rel_bias_table, lambda_q1, lambda_k1, lambda_q2, lambda_k2, attn_head_norm_g, w_attn_branch, conv_dw_w, conv_dw_b, conv_ln_g, conv_ln_b, w_conv_branch, w_out, ffn2_norm_g, ffn2_w_gate, ffn2_w_up, ffn2_w_down, final_norm_g):
    batch, seq, d = x.shape
    depth = ffn1_w_gate.shape[0]
    attn_w = N_HEADS * HEAD_W
    conv_ch = conv_dw_w.shape[2]
    xt = x.reshape(batch * seq, d)
    row = lambda v: v.reshape(1, -1).astype(F32)

    for l in range(depth):
        lam_init = 0.8 - 0.6 * math.exp(-0.3 * l)
        last = l == depth - 1
        qk_cols = (0, 2 * attn_w)
        glu_cols = (3 * attn_w, 3 * attn_w + 2 * conv_ch)
        gate_cols = (glu_cols[1], w_in.shape[2])
        w_in_bf = w_in[l].astype(BF16)
        w_vt = w_in[l][:, 2 * attn_w:3 * attn_w].T.astype(BF16)

        x1, h2 = _ffn(xt, row(ffn1_norm_g[l]), ffn1_w_gate[l].astype(BF16),
                      ffn1_w_up[l].astype(BF16), ffn1_w_down[l].astype(BF16),
                      row(mix_norm_g[l]), final=False)

        qk = _mm(h2, w_in_bf, BF16, cols=qk_cols, tile_scales=(HEAD_DIM ** -0.5, LOG2E),
                 name="proj_qk")
        vt = _mm_nt(w_vt, h2, BF16, name="proj_vt")
        glu = _mm(h2, w_in_bf, F32, cols=glu_cols, name="proj_glu")
        conv_w = jnp.repeat(conv_dw_w[l].astype(F32), SUBLANES, axis=0)
        gates, conv_u = _gates_conv(h2, w_in_bf, glu, conv_w, row(conv_dw_b[l]),
                                    row(conv_ln_g[l]), row(conv_ln_b[l]), cols=gate_cols,
                                    seq=seq)

        later = (w_attn_branch[l], w_conv_branch[l], w_out[l],
                 ffn2_w_gate[l], ffn2_w_up[l], ffn2_w_down[l])
        lam_vecs = jnp.stack([lambda_q1[l], lambda_k1[l], lambda_q2[l], lambda_k2[l]]).astype(F32)
        attn_o, w_a, w_c, w_o, w2_gate, w2_up, w2_down = _attention(
            qk, vt, rel_bias_table.astype(F32), lam_vecs,
            attn_head_norm_g[l].reshape(HEAD_W, 1).astype(F32),
            batch=batch, seq=seq, lam_init=lam_init, casts=tuple(w.astype(F32) for w in later))
        x2 = _merge(x1, attn_o, conv_u, gates, w_a, w_c, w_o)

        xt = _ffn(x2, row(ffn2_norm_g[l]), w2_gate, w2_up, w2_down, row(final_norm_g),
                  final=last)[0]

    return xt.reshape(batch, seq, d)
```
